```python
import jax
import jax.numpy as jnp
from jax import lax
import numpy as np

D_MODEL = 1024
BATCH = 8
SEQ = 2048
DEPTH = 2
DEC_BATCH = 32
DEC_SEQ = 4
PAST_LEN = 16384
PAGE_SIZE = 128

POOL_WINDOWS = (2, 4, 8, 16)
N_POOL_GROUPS = len(POOL_WINDOWS)
POOL_GROUP_DIM = D_MODEL // 16
D_POOL = N_POOL_GROUPS * POOL_GROUP_DIM
POOL_BUF = max(POOL_WINDOWS) - 1
N_FOX_HEADS = 4
FOX_HEAD_DIM = D_MODEL // 16
N_MLA_HEADS = 4
MLA_Q_LORA = D_MODEL // 4
MLA_KV_LORA = D_MODEL // 8
MLA_NOPE = D_MODEL // 16
MLA_ROPE = D_MODEL // 32
MLA_V = D_MODEL // 16
ROPE_BASE = 10000.0
N_SB_HEADS = 4
SB_HEAD_DIM = D_MODEL // 16
N_BRANCH = 4
D_BRANCH = D_MODEL // 4
N_EXPERTS = 32
TOP_K = 4
D_EXPERT = D_MODEL
SWIGLU_ALPHA = 1.702
SWIGLU_LIMIT = 7.0
MOE_BLOCK = 128
Q_BLOCK = 128
LN_EPS = 1e-5
RMS_EPS = 1e-6
DEEPNORM_ALPHA = (2 * DEPTH) ** 0.25
DEEPNORM_BETA = (8 * DEPTH) ** -0.25
IN_WIDTHS = (D_POOL, 3 * N_FOX_HEADS * FOX_HEAD_DIM, N_FOX_HEADS, MLA_Q_LORA, MLA_KV_LORA, MLA_ROPE, 3 * N_SB_HEADS * SB_HEAD_DIM, N_BRANCH * D_MODEL)
IN_SPLITS = tuple(sum(IN_WIDTHS[:i + 1]) for i in range(len(IN_WIDTHS) - 1))
D_IN = sum(IN_WIDTHS)
F32 = jnp.float32

kernel_name = 'hybrid_pool_fox_mla_stickbreak_moe_step'


def _layer_norm(x, g, b):
    xf = x.astype(F32)
    mu = jnp.mean(xf, -1, keepdims=True)
    var = jnp.mean(jnp.square(xf - mu), -1, keepdims=True)
    return ((xf - mu) * lax.rsqrt(var + LN_EPS)).astype(x.dtype) * g + b


def _rms_norm(x, g):
    xf = x.astype(F32)
    return (xf * lax.rsqrt(jnp.mean(jnp.square(xf), -1, keepdims=True) + RMS_EPS)).astype(x.dtype) * g


def _rope(x, pos):
    half = x.shape[-1] // 2
    inv = ROPE_BASE ** (-jnp.arange(half, dtype=F32) / half)
    ang = pos.astype(F32)[:, None] * inv[None, :]
    shape = (1, pos.shape[0]) + (1,) * (x.ndim - 3) + (half,)
    cos, sin = jnp.cos(ang).reshape(shape), jnp.sin(ang).reshape(shape)
    x1, x2 = x[..., :half].astype(F32), x[..., half:].astype(F32)
    return jnp.concatenate([x1 * cos - x2 * sin, x2 * cos + x1 * sin], -1).astype(x.dtype)


def _sweep(fn, qs, pos):
    t = pos.shape[0]
    if t <= Q_BLOCK or t % Q_BLOCK:
        return fn(qs, pos)
    nb = t // Q_BLOCK
    blocks = tuple(jnp.moveaxis(a.reshape((a.shape[0], nb, Q_BLOCK) + a.shape[2:]), 1, 0) for a in qs)
    out = lax.map(lambda args: fn(args[0], args[1]), (blocks, pos.reshape(nb, Q_BLOCK)))
    out = jnp.moveaxis(out, 0, 1)
    return out.reshape((out.shape[0], t) + out.shape[3:])


def _pool_mixer(u, prefix, pos, w_mix, scale):
    b, t, _ = u.shape
    z = jnp.concatenate([prefix, u], axis=1)
    cs = jnp.concatenate([jnp.zeros((b, 1, D_POOL), F32), lax.cumsum(z.astype(F32), axis=1)], axis=1)
    means = []
    for g, w in enumerate(POOL_WINDOWS):
        lo, hi = g * POOL_GROUP_DIM, (g + 1) * POOL_GROUP_DIM
        win = cs[:, POOL_BUF + 1:, lo:hi] - cs[:, POOL_BUF + 1 - w:POOL_BUF + 1 - w + t, lo:hi]
        cnt = jnp.minimum(w, pos + 1).astype(F32)
        means.append(win / cnt[None, :, None])
    pooled = jnp.concatenate(means, axis=-1).astype(u.dtype) - u
    mixed = jnp.einsum('btgc,gcd->btgd', pooled.reshape(b, t, N_POOL_GROUPS, POOL_GROUP_DIM), w_mix)
    return mixed.reshape(b, t, D_POOL) * scale, z[:, -POOL_BUF:]


def _fox_mixer(u_fox, f_logit, b_f, pos, past_kv, past_logf):
    b, t, _ = u_fox.shape
    p = past_kv.shape[1]
    qkv = u_fox.reshape(b, t, 3, N_FOX_HEADS, FOX_HEAD_DIM)
    q, new_kv = qkv[:, :, 0], qkv[:, :, 1:]
    new_lf = jax.nn.log_sigmoid((f_logit + b_f).astype(F32))
    kv = jnp.concatenate([past_kv, new_kv], axis=1)
    k, v = kv[:, :, 0], kv[:, :, 1]
    lf = jnp.concatenate([past_logf.astype(F32), new_lf], axis=1)
    r = lax.cumsum(lf, axis=1, reverse=True) - lf
    r_k = jnp.transpose(r, (0, 2, 1))
    k_pos = jnp.arange(p + t)
    scale = FOX_HEAD_DIM ** -0.5

    def block(qs, qp):
        qb, rq = qs
        s = jnp.einsum('bqhd,bkhd->bhqk', qb, k).astype(F32) * scale
        s = s + r_k[:, :, None, :] - jnp.transpose(rq, (0, 2, 1))[:, :, :, None]
        s = jnp.where(k_pos[None, :] <= qp[:, None], s, -jnp.inf)
        pr = jax.nn.softmax(s, axis=-1)
        return jnp.einsum('bhqk,bkhd->bqhd', pr.astype(v.dtype), v)

    o = _sweep(block, (q, r[:, p:]), pos)
    return o.reshape(b, t, N_FOX_HEADS * FOX_HEAD_DIM), new_kv, new_lf


def _mla_mixer(u_cq, u_ckv, u_kr, pos, past_lat, g_q, w_uq, g_kv, w_uk, w_uv):
    b, t, _ = u_cq.shape
    p = past_lat.shape[1]
    q = jnp.einsum('btc,chd->bthd', _rms_norm(u_cq, g_q), w_uq)
    q_nope, q_rope = q[..., :MLA_NOPE], _rope(q[..., MLA_NOPE:], pos)
    new_lat = jnp.concatenate([_rms_norm(u_ckv, g_kv), _rope(u_kr, pos)], axis=-1)
    lat = jnp.concatenate([past_lat, new_lat], axis=1)
    c_kv = lat[..., :MLA_KV_LORA]
    q_cat = jnp.concatenate([jnp.einsum('bthn,chn->bthc', q_nope, w_uk), q_rope], axis=-1)
    k_pos = jnp.arange(p + t)
    scale = (MLA_NOPE + MLA_ROPE) ** -0.5

    def block(qs, qp):
        (qb,) = qs
        s = jnp.einsum('bqhc,bkc->bhqk', qb, lat).astype(F32) * scale
        s = jnp.where(k_pos[None, :] <= qp[:, None], s, -jnp.inf)
        pr = jax.nn.softmax(s, axis=-1)
        return jnp.einsum('bhqk,bkc->bqhc', pr.astype(c_kv.dtype), c_kv)

    o_lat = _sweep(block, (q_cat,), pos)
    o = jnp.einsum('bthc,chv->bthv', o_lat, w_uv)
    return o.reshape(b, t, N_MLA_HEADS * MLA_V), new_lat


def _sb_mixer(u_sb, pos, past_kv):
    b, t, _ = u_sb.shape
    p = past_kv.shape[1]
    qkv = u_sb.reshape(b, t, 3, N_SB_HEADS, SB_HEAD_DIM)
    q, new_kv = qkv[:, :, 0], qkv[:, :, 1:]
    kv = jnp.concatenate([past_kv, new_kv], axis=1)
    k, v = kv[:, :, 0], kv[:, :, 1]
    k_pos = jnp.arange(p + t)
    scale = SB_HEAD_DIM ** -0.5

    def block(qs, qp):
        (qb,) = qs
        z = jnp.einsum('bqhd,bkhd->bhqk', qb, k).astype(F32) * scale
        before = k_pos[None, :] < qp[:, None]
        log_fail = jnp.where(before, jax.nn.log_sigmoid(-z), 0.0)
        log_fail_after = lax.cumsum(log_fail, axis=3, reverse=True) - log_fail
        w = jnp.where(before, jnp.exp(jax.nn.log_sigmoid(z) + log_fail_after), 0.0)
        return jnp.einsum('bhqk,bkhd->bqhd', w.astype(v.dtype), v)

    o = _sweep(block, (q,), pos)
    return o.reshape(b, t, N_SB_HEADS * SB_HEAD_DIM), new_kv


def _moe(h, w_router, b_router, w1, b1, w2, b2):
    b, t, d = h.shape
    n = b * t
    nk = n * TOP_K
    xf = h.reshape(n, d)
    logits = (xf @ w_router + b_router).astype(F32)
    top_v, top_i = lax.top_k(logits, TOP_K)
    probs = jax.nn.softmax(top_v, axis=-1)
    e_flat = top_i.reshape(nk)
    order = jnp.argsort(e_flat)
    sizes = jnp.bincount(e_flat, length=N_EXPERTS)
    padded = (sizes + MOE_BLOCK - 1) // MOE_BLOCK * MOE_BLOCK
    ends = jnp.cumsum(padded)
    e_sorted = e_flat[order]
    dest_sorted = (ends - padded)[e_sorted] + jnp.arange(nk) - (jnp.cumsum(sizes) - sizes)[e_sorted]
    dest = jnp.zeros(nk, jnp.int32).at[order].set(dest_sorted.astype(jnp.int32))
    n_blocks = -(-(nk + N_EXPERTS * (MOE_BLOCK - 1)) // MOE_BLOCK)
    slot_tok = jnp.zeros(n_blocks * MOE_BLOCK, jnp.int32).at[dest].set(jnp.arange(nk, dtype=jnp.int32) // TOP_K)
    block_exp = jnp.minimum(jnp.searchsorted(ends, jnp.arange(n_blocks) * MOE_BLOCK, side='right'), N_EXPERTS - 1)

    def expert_block(args):
        tok, e = args
        a = xf[tok] @ w1[e] + b1[e]
        glu = jnp.minimum(a[:, 0::2], SWIGLU_LIMIT)
        lin = jnp.clip(a[:, 1::2], -SWIGLU_LIMIT, SWIGLU_LIMIT)
        return (glu * jax.nn.sigmoid(SWIGLU_ALPHA * glu) * (lin + 1)) @ w2[e] + b2[e]

    out = lax.map(expert_block, (slot_tok.reshape(n_blocks, MOE_BLOCK), block_exp)).reshape(n_blocks * MOE_BLOCK, d)
    y = jnp.einsum('nkd,nk->nd', out[dest].reshape(n, TOP_K, d), probs.astype(out.dtype))
    return y.reshape(b, t, d)


def _layer(x, c, pool_prefix, fox_kv_past, fox_lf_past, mla_past, sb_kv_past, lw):
    b, t, _ = x.shape
    pos = fox_kv_past.shape[1] + jnp.arange(t)
    mod = jax.nn.silu(c) @ lw['w_ada'] + lw['b_ada']
    sh1, sc1, gt1, sh2, sc2, gt2 = jnp.split(mod[:, None, :], 6, axis=-1)
    h = x * (1 + sc1) + sh1
    u_pool, u_fox, f_logit, u_cq, u_ckv, u_kr, u_sb, u_gate = jnp.split(h @ lw['w_in'], IN_SPLITS, axis=-1)
    br_a, pool_new = _pool_mixer(u_pool, pool_prefix, pos, lw['w_pool_mix'], lw['pool_scale'])
    br_b, fox_kv_new, fox_lf_new = _fox_mixer(u_fox, f_logit, lw['b_fox_forget'], pos, fox_kv_past, fox_lf_past)
    br_c, mla_new = _mla_mixer(u_cq, u_ckv, u_kr, pos, mla_past, lw['mla_q_norm'], lw['w_mla_uq'], lw['mla_kv_norm'], lw['w_mla_uk'], lw['w_mla_uv'])
    br_d, sb_kv_new = _sb_mixer(u_sb, pos, sb_kv_past)
    branches = jnp.einsum('btnc,ncd->btnd', jnp.stack([br_a, br_b, br_c, br_d], axis=2), lw['w_branch'])
    gates = jax.nn.sigmoid(u_gate.reshape(b, t, N_BRANCH, D_MODEL))
    y = jnp.sum(gates * branches, axis=2) @ lw['w_out']
    x = _layer_norm(DEEPNORM_ALPHA * x + (1 + gt1) * y, lw['ln1_g'], lw['ln1_b'])
    h2 = x * (1 + sc2) + sh2
    f = _moe(h2, lw['w_router'], lw['b_router'], lw['w_exp1'], lw['b_exp1'], lw['w_exp2'], lw['b_exp2'])
    x = _layer_norm(DEEPNORM_ALPHA * x + (1 + gt2) * f, lw['ln2_g'], lw['ln2_b'])
    return x, (fox_kv_new, fox_lf_new, mla_new, sb_kv_new, pool_new)


def setup_inputs(seed: int = 0) -> dict:
    key = jax.random.key(seed)
    keys = iter(jax.random.split(key, 40))

    def nrm(shape, scale):
        return jax.random.normal(next(keys), shape, F32) * scale

    d = D_MODEL
    n_pages = PAST_LEN // PAGE_SIZE
    n_pool = DEC_BATCH * n_pages * 5 // 4
    x_prompt = nrm((BATCH, SEQ, d), 1.0)
    x_sample = nrm((DEC_BATCH, DEC_SEQ, d), 1.0)
    c_prompt = nrm((BATCH, d), 1.0)
    c_sample = nrm((DEC_BATCH, d), 1.0)
    cache_fox_kv = nrm((DEPTH, n_pool, PAGE_SIZE, 2, N_FOX_HEADS, FOX_HEAD_DIM), 1.0)
    cache_fox_logf = jax.nn.log_sigmoid(nrm((DEPTH, n_pool, PAGE_SIZE, N_FOX_HEADS), 1.0) + 3.0)
    cache_mla = nrm((DEPTH, n_pool, PAGE_SIZE, MLA_KV_LORA + MLA_ROPE), 1.0)
    cache_sb_kv = nrm((DEPTH, n_pool, PAGE_SIZE, 2, N_SB_HEADS, SB_HEAD_DIM), 1.0)
    state_pool = nrm((DEPTH, DEC_BATCH, POOL_BUF, D_POOL), 1.0)
    page_table = jax.random.permutation(next(keys), n_pool)[:DEC_BATCH * n_pages].reshape(DEC_BATCH, n_pages).astype(jnp.int32)
    return {
        'x_prompt': x_prompt, 'x_sample': x_sample, 'c_prompt': c_prompt, 'c_sample': c_sample,
        'cache_fox_kv': cache_fox_kv, 'cache_fox_logf': cache_fox_logf, 'cache_mla': cache_mla,
        'cache_sb_kv': cache_sb_kv, 'state_pool': state_pool, 'page_table': page_table,
        'w_ada': nrm((DEPTH, d, 6 * d), 0.1 * d ** -0.5),
        'b_ada': nrm((DEPTH, 6 * d), 0.01),
        'w_in': nrm((DEPTH, d, D_IN), d ** -0.5),
        'b_fox_forget': 3.0 + nrm((DEPTH, N_FOX_HEADS), 0.1),
        'w_pool_mix': nrm((DEPTH, N_POOL_GROUPS, POOL_GROUP_DIM, POOL_GROUP_DIM), POOL_GROUP_DIM ** -0.5),
        'pool_scale': 1.0 + nrm((DEPTH, D_POOL), 0.1),
        'mla_q_norm': 1.0 + nrm((DEPTH, MLA_Q_LORA), 0.1),
        'w_mla_uq': nrm((DEPTH, MLA_Q_LORA, N_MLA_HEADS, MLA_NOPE + MLA_ROPE), MLA_Q_LORA ** -0.5),
        'mla_kv_norm': 1.0 + nrm((DEPTH, MLA_KV_LORA), 0.1),
        'w_mla_uk': nrm((DEPTH, MLA_KV_LORA, N_MLA_HEADS, MLA_NOPE), MLA_NOPE ** -0.5),
        'w_mla_uv': nrm((DEPTH, MLA_KV_LORA, N_MLA_HEADS, MLA_V), MLA_KV_LORA ** -0.5),
        'w_branch': nrm((DEPTH, N_BRANCH, D_BRANCH, d), D_BRANCH ** -0.5),
        'w_out': nrm((DEPTH, d, d), DEEPNORM_BETA * d ** -0.5),
        'ln1_g': 1.0 + nrm((DEPTH, d), 0.1),
        'ln1_b': nrm((DEPTH, d), 0.01),
        'w_router': nrm((DEPTH, d, N_EXPERTS), d ** -0.5),
        'b_router': nrm((DEPTH, N_EXPERTS), 0.01),
        'w_exp1': nrm((DEPTH, N_EXPERTS, d, 2 * D_EXPERT), d ** -0.5),
        'b_exp1': nrm((DEPTH, N_EXPERTS, 2 * D_EXPERT), 0.01),
        'w_exp2': nrm((DEPTH, N_EXPERTS, D_EXPERT, d), DEEPNORM_BETA * D_EXPERT ** -0.5),
        'b_exp2': nrm((DEPTH, N_EXPERTS, d), 0.01),
        'ln2_g': 1.0 + nrm((DEPTH, d), 0.1),
        'ln2_b': nrm((DEPTH, d), 0.01),
    }


def reference(x_prompt, x_sample, c_prompt, c_sample, cache_fox_kv, cache_fox_logf, cache_mla, cache_sb_kv,
              state_pool, page_table, w_ada, b_ada, w_in, b_fox_forget, w_pool_mix, pool_scale, mla_q_norm,
              w_mla_uq, mla_kv_norm, w_mla_uk, w_mla_uv, w_branch, w_out, ln1_g, ln1_b, w_router, b_router,
              w_exp1, b_exp1, w_exp2, b_exp2, ln2_g, ln2_b):
    dec_b, n_pages = page_table.shape

    def gather(pool):
        rows = pool[page_table]
        return rows.reshape((dec_b, n_pages * pool.shape[1]) + pool.shape[2:])

    bp, dt = x_prompt.shape[0], x_prompt.dtype
    xp, xs = x_prompt, x_sample
    st_p, st_s = [], []
    for l in range(DEPTH):
        lw = {
            'w_ada': w_ada[l], 'b_ada': b_ada[l], 'w_in': w_in[l], 'b_fox_forget': b_fox_forget[l],
            'w_pool_mix': w_pool_mix[l], 'pool_scale': pool_scale[l], 'mla_q_norm': mla_q_norm[l],
            'w_mla_uq': w_mla_uq[l], 'mla_kv_norm': mla_kv_norm[l], 'w_mla_uk': w_mla_uk[l],
            'w_mla_uv': w_mla_uv[l], 'w_branch': w_branch[l], 'w_out': w_out[l], 'ln1_g': ln1_g[l],
            'ln1_b': ln1_b[l], 'w_router': w_router[l], 'b_router': b_router[l], 'w_exp1': w_exp1[l],
            'b_exp1': b_exp1[l], 'w_exp2': w_exp2[l], 'b_exp2': b_exp2[l], 'ln2_g': ln2_g[l], 'ln2_b': ln2_b[l],
        }
        xp, sp = _layer(xp, c_prompt,
                        jnp.zeros((bp, POOL_BUF, D_POOL), dt),
                        jnp.zeros((bp, 0, 2, N_FOX_HEADS, FOX_HEAD_DIM), dt),
                        jnp.zeros((bp, 0, N_FOX_HEADS), F32),
                        jnp.zeros((bp, 0, MLA_KV_LORA + MLA_ROPE), dt),
                        jnp.zeros((bp, 0, 2, N_SB_HEADS, SB_HEAD_DIM), dt), lw)
        xs, ss = _layer(xs, c_sample, state_pool[l], gather(cache_fox_kv[l]), gather(cache_fox_logf[l]),
                        gather(cache_mla[l]), gather(cache_sb_kv[l]), lw)
        st_p.append(sp)
        st_s.append(ss)
    fox_kv_prompt = jnp.stack([s[0] for s in st_p])
    fox_kv_sample = jnp.stack([s[0] for s in st_s])
    fox_logf_prompt = jnp.stack([s[1] for s in st_p])
    fox_logf_sample = jnp.stack([s[1] for s in st_s])
    mla_prompt = jnp.stack([s[2] for s in st_p])
    mla_sample = jnp.stack([s[2] for s in st_s])
    sb_kv_prompt = jnp.stack([s[3] for s in st_p])
    sb_kv_sample = jnp.stack([s[3] for s in st_s])
    pool_prompt = jnp.stack([s[4] for s in st_p])
    pool_sample = jnp.stack([s[4] for s in st_s])
    return (xp, xs, fox_kv_prompt, fox_kv_sample, fox_logf_prompt, fox_logf_sample, mla_prompt, mla_sample,
            sb_kv_prompt, sb_kv_sample, pool_prompt, pool_sample)
```

```python
import functools

import jax
import jax.numpy as jnp
from jax import lax
from jax.experimental import pallas as pl
from jax.experimental.pallas import tpu as pltpu

F32 = jnp.float32
BF16 = jnp.bfloat16

POOL_WINDOWS = (2, 4, 8, 16)
POOL_BUF = max(POOL_WINDOWS) - 1
N_HEADS = 4
N_BRANCH = 4
N_MOD = 6
TOP_K = 4
ROPE_BASE = 10000.0
SWIGLU_ALPHA = 1.702
SWIGLU_LIMIT = 7.0
LN_EPS = 1e-5
RMS_EPS = 1e-6

LANES = 128
SUBLANES = 8
VMEM_LIMIT = 56 * 1024 * 1024

NEG = -1e30

ROW_TILE = 512
ATT_TILE = 256
PAGES_PER_STEP = 8
MOE_ROWS = 256


def _cparams(n_axes):
    return pltpu.CompilerParams(dimension_semantics=("arbitrary",) * n_axes, vmem_limit_bytes=VMEM_LIMIT)


def _nt(a, b):
    return lax.dot_general(a, b, (((1,), (1,)), ((), ())), preferred_element_type=F32)


def _dot(a, b):
    return jnp.dot(a, b, preferred_element_type=F32)


def _split3(x):
    hi = x.astype(BF16)
    r = x - hi.astype(F32)
    mid = r.astype(BF16)
    lo = (r - mid.astype(F32)).astype(BF16)
    return hi, mid, lo


def _dot_f32_lhs(x, m01):
    hi, mid, lo = _split3(x)
    return _dot(hi, m01) + _dot(mid, m01) + _dot(lo, m01)


def _dot_f32_rhs(m01, x):
    hi, mid, lo = _split3(x)
    return _dot(m01, hi) + _dot(m01, mid) + _dot(m01, lo)


def _log_sigmoid(x):
    return jnp.minimum(x, 0.0) - jnp.log1p(jnp.exp(-jnp.abs(x)))


def _mod_spec(mod, tm, rows_per_seq):
    d = mod.shape[-1]
    if mod.shape[2] == 1:
        return pl.BlockSpec((None, N_MOD, 1, d), lambda i: ((i * tm) // rows_per_seq, 0, 0, 0))
    return pl.BlockSpec((None, N_MOD, tm, d), lambda i: (0, 0, i, 0))


def _ada_kernel(c_ref, w_ref, b_ref, o_ref):
    c = c_ref[...]
    s = c * jax.nn.sigmoid(c)
    o_ref[...] = _dot(s.astype(BF16), w_ref[...].astype(BF16)) + b_ref[...]


def _ada(c_all, w_ada, b_ada):
    depth, d, _ = w_ada.shape
    r = c_all.shape[0]
    b4 = b_ada.reshape(depth, N_MOD, 1, d)
    return pl.pallas_call(
        _ada_kernel,
        grid=(depth, N_MOD),
        in_specs=[pl.BlockSpec((r, d), lambda l, k: (0, 0)),
                  pl.BlockSpec((None, d, d), lambda l, k: (l, 0, k)),
                  pl.BlockSpec((None, None, 1, d), lambda l, k: (l, k, 0, 0))],
        out_specs=pl.BlockSpec((None, None, r, d), lambda l, k: (l, k, 0, 0)),
        out_shape=jax.ShapeDtypeStruct((depth, N_MOD, r, d), F32),
        compiler_params=_cparams(2),
    )(c_all, w_ada, b4)


def _inproj_kernel(x_ref, mod_ref, w_ref, *out_refs, widths):
    h = x_ref[...] * (1.0 + mod_ref[1]) + mod_ref[0]
    u = _dot(h.astype(BF16), w_ref[...])
    off = 0
    for o_ref, w in zip(out_refs, widths):
        o_ref[...] = u[:, off:off + w]
        off += w


def _inproj(x2d, mod, w_a, widths, tm, rows_per_seq):
    n, d = x2d.shape
    return pl.pallas_call(
        functools.partial(_inproj_kernel, widths=widths),
        grid=(n // tm,),
        in_specs=[pl.BlockSpec((tm, d), lambda i: (i, 0)),
                  _mod_spec(mod, tm, rows_per_seq),
                  pl.BlockSpec(w_a.shape, lambda i: (0, 0))],
        out_specs=[pl.BlockSpec((tm, w), lambda i: (i, 0)) for w in widths],
        out_shape=[jax.ShapeDtypeStruct((n, w), F32) for w in widths],
        compiler_params=_cparams(1),
    )(x2d, mod, w_a)


def _pool_kernel(u_ref, pre_ref, w_ref, sc_ref, o_ref, new_ref, z_ref, *, t, pos0):
    total, dp = z_ref.shape
    base = POOL_BUF + 1
    z_ref[0:1, :] = jnp.zeros((1, dp), F32)
    z_ref[1:base, :] = pre_ref[...]
    z_ref[base:base + t, :] = u_ref[...]
    if total > base + t:
        z_ref[base + t:total, :] = jnp.zeros((total - base - t, dp), F32)
    z = z_ref[...]
    sums = []
    s = z
    for step in (1, 2, 4, 8):
        s = s + pltpu.roll(s, step, 0)
        sums.append(s)
    lane = lax.broadcasted_iota(jnp.int32, (t, dp), 1)
    group = dp // len(POOL_WINDOWS)
    win = sums[-1][base:base + t]
    wlen = jnp.full((t, dp), float(POOL_WINDOWS[-1]), F32)
    for g in range(len(POOL_WINDOWS) - 2, -1, -1):
        sel = lane < (g + 1) * group
        win = jnp.where(sel, sums[g][base:base + t], win)
        wlen = jnp.where(sel, float(POOL_WINDOWS[g]), wlen)
    pos = (pos0 + lax.broadcasted_iota(jnp.int32, (t, dp), 0)).astype(F32)
    cnt = jnp.minimum(wlen, pos + 1.0)
    pooled = win / cnt - u_ref[...]
    o_ref[...] = (_dot(pooled.astype(BF16), w_ref[...]) * sc_ref[...]).astype(o_ref.dtype)
    new_ref[...] = z_ref[t + 1:t + 1 + POOL_BUF, :]


def _pool(u_pool, prefix, w_bd, scale, pos0):
    b, t, dp = u_pool.shape
    total = -(-(POOL_BUF + 1 + t) // SUBLANES) * SUBLANES
    return pl.pallas_call(
        functools.partial(_pool_kernel, t=t, pos0=pos0),
        grid=(b,),
        in_specs=[pl.BlockSpec((None, t, dp), lambda i: (i, 0, 0)),
                  pl.BlockSpec((None, POOL_BUF, dp), lambda i: (i, 0, 0)),
                  pl.BlockSpec((dp, dp), lambda i: (0, 0)),
                  pl.BlockSpec((1, dp), lambda i: (0, 0))],
        out_specs=[pl.BlockSpec((None, t, dp), lambda i: (i, 0, 0)),
                   pl.BlockSpec((None, POOL_BUF, dp), lambda i: (i, 0, 0))],
        out_shape=[jax.ShapeDtypeStruct((b, t, dp), BF16),
                   jax.ShapeDtypeStruct((b, POOL_BUF, dp), F32)],
        scratch_shapes=[pltpu.VMEM((total, dp), F32)],
        compiler_params=_cparams(1),
    )(u_pool, prefix, w_bd, scale)


def _fox_prep_kernel(s_ref, b_ref, lf_ref, c_ref, *, t, col):
    x = s_ref[...] + b_ref[...]
    lf = _log_sigmoid(x)
    lf_ref[...] = lf[:, col:col + N_HEADS]
    if t % LANES == 0:
        r = lax.broadcasted_iota(jnp.int32, (LANES, LANES), 0)
        c = lax.broadcasted_iota(jnp.int32, (LANES, LANES), 1)
        tri = jnp.where(r >= c, 1.0, 0.0).astype(BF16)
        carry = jnp.zeros((1, LANES), F32)
        for ch in range(t // LANES):
            cs = _dot_f32_rhs(tri, lf[ch * LANES:(ch + 1) * LANES]) + carry
            c_ref[ch * LANES:(ch + 1) * LANES, :] = cs[:, col:col + N_HEADS]
            carry = cs[LANES - 1:LANES, :]
    else:
        acc = jnp.zeros((1, LANES), F32)
        for i in range(t):
            acc = acc + lf[i:i + 1]
            c_ref[i:i + 1, :] = acc[:, col:col + N_HEADS]


def _fox_prep(small, bias_row, col):
    b, t, w = small.shape
    return pl.pallas_call(
        functools.partial(_fox_prep_kernel, t=t, col=col),
        grid=(b,),
        in_specs=[pl.BlockSpec((None, t, w), lambda i: (i, 0, 0)),
                  pl.BlockSpec((1, w), lambda i: (0, 0))],
        out_specs=[pl.BlockSpec((None, t, N_HEADS), lambda i: (i, 0, 0))] * 2,
        out_shape=[jax.ShapeDtypeStruct((b, t, N_HEADS), F32)] * 2,
        compiler_params=_cparams(1),
    )(small, bias_row)


def _softmax_step(carry, s, v):
    m, l, acc = carry
    m_new = jnp.maximum(m, jnp.max(s, axis=-1, keepdims=True))
    p = jnp.exp(s - m_new)
    alpha = jnp.exp(m - m_new)
    l = alpha * l + jnp.sum(p, axis=-1, keepdims=True)
    acc = alpha * acc + _dot(p.astype(BF16), v)
    return m_new, l, acc


def _causal_mask(tq, tk, strict):
    r = lax.broadcasted_iota(jnp.int32, (tq, tk), 0)
    c = lax.broadcasted_iota(jnp.int32, (tq, tk), 1)
    return (c < r) if strict else (c <= r)


def _fox_attn_kernel(q_ref, k_ref, v_ref, ct_ref, cs_ref, o_ref, *, tile, dh):
    i = pl.program_id(1)
    scale = dh ** -0.5
    mask = _causal_mask(tile, tile, False)
    for h in range(N_HEADS):
        sl = slice(h * dh, (h + 1) * dh)
        q = (q_ref[:, sl] * scale).astype(BF16)
        ct = ct_ref[:, h:h + 1]

        def scores(j, h=h, sl=sl, q=q, ct=ct):
            rows = pl.ds(pl.multiple_of(j * tile, tile), tile)
            k = k_ref[rows, sl].astype(BF16)
            v = v_ref[rows, sl].astype(BF16)
            return _nt(q, k) + (ct - cs_ref[h, j]), v

        def body(j, carry):
            s, v = scores(j)
            return _softmax_step(carry, s, v)

        init = (jnp.full((tile, 1), NEG, F32), jnp.zeros((tile, 1), F32), jnp.zeros((tile, dh), F32))
        carry = lax.fori_loop(0, i, body, init)
        s, v = scores(i)
        m, l, acc = _softmax_step(carry, jnp.where(mask, s, NEG), v)
        o_ref[:, sl] = (acc / l).astype(o_ref.dtype)


def _fox_attn(u_fox, c_t, c_s, tile):
    b, t, w = u_fox.shape
    hd = w // 3
    dh = hd // N_HEADS
    nq = t // tile
    return pl.pallas_call(
        functools.partial(_fox_attn_kernel, tile=tile, dh=dh),
        grid=(b, nq),
        in_specs=[pl.BlockSpec((None, tile, hd), lambda bi, i: (bi, i, 0)),
                  pl.BlockSpec((None, t, hd), lambda bi, i: (bi, 0, 1)),
                  pl.BlockSpec((None, t, hd), lambda bi, i: (bi, 0, 2)),
                  pl.BlockSpec((None, tile, N_HEADS), lambda bi, i: (bi, i, 0)),
                  pl.BlockSpec((None, N_HEADS, nq, 1, tile), lambda bi, i: (bi, 0, 0, 0, 0))],
        out_specs=pl.BlockSpec((None, tile, hd), lambda bi, i: (bi, i, 0)),
        out_shape=jax.ShapeDtypeStruct((b, t, hd), BF16),
        compiler_params=_cparams(2),
    )(u_fox, u_fox, u_fox, c_t, c_s)


def _mla_attn_kernel(q_ref, lat_ref, wuv_ref, o_ref, *, tile, lora, scale):
    i = pl.program_id(1)
    mask = _causal_mask(tile, tile, False)
    wq = q_ref.shape[1] // N_HEADS
    outs = []
    for h in range(N_HEADS):
        q = q_ref[:, h * wq:(h + 1) * wq]

        def scores(j, q=q):
            rows = pl.ds(pl.multiple_of(j * tile, tile), tile)
            lat = lat_ref[rows, :]
            return _nt(q, lat) * scale, lat[:, :lora]

        def body(j, carry):
            s, v = scores(j)
            return _softmax_step(carry, s, v)

        init = (jnp.full((tile, 1), NEG, F32), jnp.zeros((tile, 1), F32), jnp.zeros((tile, lora), F32))
        carry = lax.fori_loop(0, i, body, init)
        s, v = scores(i)
        m, l, acc = _softmax_step(carry, jnp.where(mask, s, NEG), v)
        outs.append((acc / l).astype(BF16))
    o_ref[...] = _dot(jnp.concatenate(outs, axis=-1), wuv_ref[...]).astype(o_ref.dtype)


def _mla_attn(q_cat, lat_pad, w_uv_bd, tile, lora, scale):
    b, t, wq = q_cat.shape
    wl = lat_pad.shape[-1]
    wo = w_uv_bd.shape[1]
    return pl.pallas_call(
        functools.partial(_mla_attn_kernel, tile=tile, lora=lora, scale=scale),
        grid=(b, t // tile),
        in_specs=[pl.BlockSpec((None, tile, wq), lambda bi, i: (bi, i, 0)),
                  pl.BlockSpec((None, t, wl), lambda bi, i: (bi, 0, 0)),
                  pl.BlockSpec(w_uv_bd.shape, lambda bi, i: (0, 0))],
        out_specs=pl.BlockSpec((None, tile, wo), lambda bi, i: (bi, i, 0)),
        out_shape=jax.ShapeDtypeStruct((b, t, wo), BF16),
        compiler_params=_cparams(2),
    )(q_cat, lat_pad, w_uv_bd)


def _suffix_matrix(n, with_total):
    j = lax.broadcasted_iota(jnp.int32, (n, n), 0)
    s = lax.broadcasted_iota(jnp.int32, (n, n), 1)
    m = jnp.where(j > s, 1.0, 0.0).astype(BF16)
    if with_total:
        m = jnp.concatenate([m, jnp.ones((n, n), BF16)], axis=1)
    return m


def _sb_block(z, v, carry_a, acc, msuf, valid):
    n = z.shape[1]
    ls = _log_sigmoid(z)
    lfail = ls - z
    if valid is not None:
        lfail = jnp.where(valid, lfail, 0.0)
    ct = _dot_f32_lhs(lfail, msuf)
    w = jnp.exp(ls + ct[:, :n] + carry_a)
    if valid is not None:
        w = jnp.where(valid, w, 0.0)
    acc = acc + _dot(w.astype(BF16), v)
    return carry_a + ct[:, n:], acc


def _sb_attn_kernel(q_ref, k_ref, v_ref, o_ref, *, tile, dh):
    i = pl.program_id(1)
    scale = dh ** -0.5
    valid = _causal_mask(tile, tile, True)
    msuf = _suffix_matrix(tile, True)
    for h in range(N_HEADS):
        sl = slice(h * dh, (h + 1) * dh)
        q = (q_ref[:, sl] * scale).astype(BF16)

        def kv(j, sl=sl):
            rows = pl.ds(pl.multiple_of(j * tile, tile), tile)
            return k_ref[rows, sl].astype(BF16), v_ref[rows, sl].astype(BF16)

        k, v = kv(i)
        carry = _sb_block(_nt(q, k), v, jnp.zeros((tile, tile), F32), jnp.zeros((tile, dh), F32), msuf, valid)

        def body(step, carry, q=q, kv=kv):
            k, v = kv(i - 1 - step)
            return _sb_block(_nt(q, k), v, carry[0], carry[1], msuf, None)

        carry = lax.fori_loop(0, i, body, carry)
        o_ref[:, sl] = carry[1].astype(o_ref.dtype)


def _sb_attn(u_sb, tile):
    b, t, w = u_sb.shape
    hd = w // 3
    dh = hd // N_HEADS
    return pl.pallas_call(
        functools.partial(_sb_attn_kernel, tile=tile, dh=dh),
        grid=(b, t // tile),
        in_specs=[pl.BlockSpec((None, tile, hd), lambda bi, i: (bi, i, 0)),
                  pl.BlockSpec((None, t, hd), lambda bi, i: (bi, 0, 1)),
                  pl.BlockSpec((None, t, hd), lambda bi, i: (bi, 0, 2))],
        out_specs=pl.BlockSpec((None, tile, hd), lambda bi, i: (bi, i, 0)),
        out_shape=jax.ShapeDtypeStruct((b, t, hd), BF16),
        compiler_params=_cparams(2),
    )(u_sb, u_sb, u_sb)


def _rms(x, g):
    return x * lax.rsqrt(jnp.mean(jnp.square(x), axis=-1, keepdims=True) + RMS_EPS) * g


def _mla_prep_kernel(u_ref, s_ref, cos_ref, sin_ref, gq_ref, gkv_ref, wn_ref, wr_ref, wrs_ref, wuk_ref,
                     qcat_ref, latpad_ref, lat_ref, *, q_lora, kv_lora, rope):
    cos = cos_ref[...]
    sin = sin_ref[...]
    cq = _rms(u_ref[:, :q_lora], gq_ref[...]).astype(BF16)
    q_nope = _dot(cq, wn_ref[...])
    q_rope = _dot(cq, wr_ref[...])
    q_rope_sw = _dot(cq, wrs_ref[...])
    q_abs = _dot(q_nope.astype(BF16), wuk_ref[...])
    pieces = []
    for h in range(N_HEADS):
        blk = slice(h * LANES, (h + 1) * LANES)
        pieces.append(q_abs[:, h * kv_lora:(h + 1) * kv_lora])
        pieces.append(q_rope[:, blk] * cos + q_rope_sw[:, blk] * sin)
    qcat_ref[...] = jnp.concatenate(pieces, axis=-1).astype(qcat_ref.dtype)
    ckv = _rms(u_ref[:, q_lora:q_lora + kv_lora], gkv_ref[...])
    small = s_ref[...]
    kr = small * cos + pltpu.roll(small, LANES - rope, 1) * sin
    lane = lax.broadcasted_iota(jnp.int32, kr.shape, 1)
    kr = jnp.where(lane < rope, kr, 0.0)
    latpad_ref[...] = jnp.concatenate([ckv, kr], axis=-1).astype(latpad_ref.dtype)
    lat_ref[...] = jnp.concatenate([ckv, kr[:, :rope]], axis=-1)


def _mla_prep(u_mla, small, cos, sin, g_q, g_kv, w_nope, w_rope, w_rope_sw, w_uk_bd, tm, rope):
    n, wm = u_mla.shape
    q_lora = g_q.shape[1]
    kv_lora = g_kv.shape[1]
    ncs = cos.shape[0] // tm
    wq = N_HEADS * (kv_lora + LANES)
    full = lambda a: pl.BlockSpec(a.shape, lambda i: (0, 0))
    return pl.pallas_call(
        functools.partial(_mla_prep_kernel, q_lora=q_lora, kv_lora=kv_lora, rope=rope),
        grid=(n // tm,),
        in_specs=[pl.BlockSpec((tm, wm), lambda i: (i, 0)),
                  pl.BlockSpec((tm, LANES), lambda i: (i, 0)),
                  pl.BlockSpec((tm, LANES), lambda i: (i % ncs, 0)),
                  pl.BlockSpec((tm, LANES), lambda i: (i % ncs, 0)),
                  full(g_q), full(g_kv), full(w_nope), full(w_rope), full(w_rope_sw), full(w_uk_bd)],
        out_specs=[pl.BlockSpec((tm, wq), lambda i: (i, 0)),
                   pl.BlockSpec((tm, kv_lora + LANES), lambda i: (i, 0)),
                   pl.BlockSpec((tm, kv_lora + rope), lambda i: (i, 0))],
        out_shape=[jax.ShapeDtypeStruct((n, wq), BF16),
                   jax.ShapeDtypeStruct((n, kv_lora + LANES), BF16),
                   jax.ShapeDtypeStruct((n, kv_lora + rope), F32)],
        compiler_params=_cparams(1),
    )(u_mla, small, cos, sin, g_q, g_kv, w_nope, w_rope, w_rope_sw, w_uk_bd)


def _page_specs(width, n_pages, pps):
    def spec(r):
        return pl.BlockSpec((None, width[0], width[1]),
                            lambda b, c, pt, r=r: (pt[b * n_pages + (n_pages - 1 - (c * pps + r))], 0, 0))
    return [spec(r) for r in range(pps)]


def _row_head(rows):
    return lax.broadcasted_iota(jnp.int32, (rows, 1), 0) // (rows // N_HEADS)


def _new_token_mask(rows, n_new, strict):
    r = lax.broadcasted_iota(jnp.int32, (rows, LANES), 0) % (rows // N_HEADS)
    c = lax.broadcasted_iota(jnp.int32, (rows, LANES), 1)
    limit = jnp.minimum(r if strict else r + 1, n_new)
    return c < limit


def _fox_sfx_kernel(x_ref, m_ref, o_ref):
    o_ref[...] = _dot_f32_lhs(x_ref[...], m_ref[...])


def _fox_sfx(logf_pages, page):
    npg, w = logf_pages.shape
    a = jnp.arange(w)
    o = jnp.arange(w)
    same = (a[:, None] % N_HEADS) == (o[None, :] // page)
    later = (a[:, None] // N_HEADS) > (o[None, :] % page)
    m = jnp.concatenate([same & later, same], axis=1).astype(BF16)
    tp = 512 if npg % 512 == 0 else npg
    return pl.pallas_call(
        _fox_sfx_kernel,
        grid=(npg // tp,),
        in_specs=[pl.BlockSpec((tp, w), lambda i: (i, 0)), pl.BlockSpec(m.shape, lambda i: (0, 0))],
        out_specs=pl.BlockSpec((tp, 2 * w), lambda i: (i, 0)),
        out_shape=jax.ShapeDtypeStruct((npg, 2 * w), F32),
        compiler_params=_cparams(1),
    )(logf_pages, m)


def _fox_dec_kernel(pt_ref, q_ref, kvn_ref, gn_ref, *rest, pps, n_new, hd):
    kv_refs, sfx_refs = rest[:pps], rest[pps:2 * pps]
    o_ref, m_ref, l_ref, acc_ref, car_ref = rest[2 * pps:]
    c = pl.program_id(1)
    q = q_ref[...]
    rows = q.shape[0]
    rh = _row_head(rows)

    @pl.when(c == 0)
    def _():
        s = _nt(q, kvn_ref[:, :hd].astype(BF16)) + gn_ref[...]
        s = jnp.where(_new_token_mask(rows, n_new, False), s, NEG)
        m = jnp.max(s, axis=-1, keepdims=True)
        p = jnp.exp(s - m)
        m_ref[...] = m
        l_ref[...] = jnp.sum(p, axis=-1, keepdims=True)
        acc_ref[...] = _dot(p.astype(BF16), kvn_ref[:, hd:].astype(BF16))
        car_ref[...] = jnp.zeros(car_ref.shape, F32)

    ss, vs = [], []
    for r in range(pps):
        row = sfx_refs[r][...]
        g = None
        for h in range(N_HEADS):
            gh = row[:, h * LANES:(h + 1) * LANES] + car_ref[h:h + 1, :]
            g = gh if g is None else jnp.where(rh == h, gh, g)
            car_ref[h:h + 1, :] = car_ref[h:h + 1, :] + row[:, (N_HEADS + h) * LANES:(N_HEADS + h + 1) * LANES]
        ss.append(_nt(q, kv_refs[r][:, :hd].astype(BF16)) + g)
        vs.append(kv_refs[r][:, hd:].astype(BF16))
    m_old = m_ref[...]
    m_new = m_old
    for s in ss:
        m_new = jnp.maximum(m_new, jnp.max(s, axis=-1, keepdims=True))
    alpha = jnp.exp(m_old - m_new)
    l = alpha * l_ref[...]
    acc = alpha * acc_ref[...]
    for s, v in zip(ss, vs):
        p = jnp.exp(s - m_new)
        l = l + jnp.sum(p, axis=-1, keepdims=True)
        acc = acc + _dot(p.astype(BF16), v)
    m_ref[...] = m_new
    l_ref[...] = l
    acc_ref[...] = acc

    @pl.when(c == pl.num_programs(1) - 1)
    def _():
        o_ref[...] = acc / l


def _fox_dec(q_rows, kv_new, g_new, cache_kv, sfx, page_table, n_new):
    db, rows, hd = q_rows.shape
    n_pages = page_table.shape[1]
    pps = min(PAGES_PER_STEP, n_pages)
    page, w = cache_kv.shape[1:]
    ws = sfx.shape[-1]
    pt = page_table.reshape(-1)
    grid_spec = pltpu.PrefetchScalarGridSpec(
        num_scalar_prefetch=1,
        grid=(db, n_pages // pps),
        in_specs=[pl.BlockSpec((None, rows, hd), lambda b, c, pt: (b, 0, 0)),
                  pl.BlockSpec((None, LANES, w), lambda b, c, pt: (b, 0, 0)),
                  pl.BlockSpec((None, rows, LANES), lambda b, c, pt: (b, 0, 0))]
        + _page_specs((page, w), n_pages, pps) + _page_specs((1, ws), n_pages, pps),
        out_specs=pl.BlockSpec((None, rows, hd), lambda b, c, pt: (b, 0, 0)),
        scratch_shapes=[pltpu.VMEM((rows, 1), F32), pltpu.VMEM((rows, 1), F32), pltpu.VMEM((rows, hd), F32),
                        pltpu.VMEM((SUBLANES, LANES), F32)])
    return pl.pallas_call(
        functools.partial(_fox_dec_kernel, pps=pps, n_new=n_new, hd=hd),
        grid_spec=grid_spec,
        out_shape=jax.ShapeDtypeStruct((db, rows, hd), F32),
        compiler_params=_cparams(2),
    )(pt, q_rows, kv_new, g_new, *([cache_kv] * pps), *([sfx.reshape(-1, 1, ws)] * pps))


def _mla_dec_kernel(pt_ref, q_ref, latn_ref, wuv_ref, *rest, pps, n_new, lora, scale):
    lat_refs = rest[:pps]
    o_ref, m_ref, l_ref, acc_ref = rest[pps:]
    c = pl.program_id(1)
    q = q_ref[...]
    rows = q.shape[0]

    @pl.when(c == 0)
    def _():
        lat = latn_ref[...].astype(BF16)
        s = jnp.where(_new_token_mask(rows, n_new, False), _nt(q, lat) * scale, NEG)
        m = jnp.max(s, axis=-1, keepdims=True)
        p = jnp.exp(s - m)
        m_ref[...] = m
        l_ref[...] = jnp.sum(p, axis=-1, keepdims=True)
        acc_ref[...] = _dot(p.astype(BF16), lat[:, :lora])

    ss, vs = [], []
    for r in range(pps):
        lat = lat_refs[r][...].astype(BF16)
        ss.append(_nt(q, lat) * scale)
        vs.append(lat[:, :lora])
    m_old = m_ref[...]
    m_new = m_old
    for s in ss:
        m_new = jnp.maximum(m_new, jnp.max(s, axis=-1, keepdims=True))
    alpha = jnp.exp(m_old - m_new)
    l = alpha * l_ref[...]
    acc = alpha * acc_ref[...]
    for s, v in zip(ss, vs):
        p = jnp.exp(s - m_new)
        l = l + jnp.sum(p, axis=-1, keepdims=True)
        acc = acc + _dot(p.astype(BF16), v)
    m_ref[...] = m_new
    l_ref[...] = l
    acc_ref[...] = acc

    @pl.when(c == pl.num_programs(1) - 1)
    def _():
        o_ref[...] = _dot((acc / l).astype(BF16), wuv_ref[...])


def _mla_dec(q_rows, lat_new, w_uv_all, cache_lat, page_table, n_new, lora, scale):
    db, rows, wq = q_rows.shape
    n_pages = page_table.shape[1]
    pps = min(PAGES_PER_STEP, n_pages)
    page, w = cache_lat.shape[1:]
    wo = w_uv_all.shape[1]
    pt = page_table.reshape(-1)
    grid_spec = pltpu.PrefetchScalarGridSpec(
        num_scalar_prefetch=1,
        grid=(db, n_pages // pps),
        in_specs=[pl.BlockSpec((None, rows, wq), lambda b, c, pt: (b, 0, 0)),
                  pl.BlockSpec((None, LANES, w), lambda b, c, pt: (b, 0, 0)),
                  pl.BlockSpec(w_uv_all.shape, lambda b, c, pt: (0, 0))]
        + _page_specs((page, w), n_pages, pps),
        out_specs=pl.BlockSpec((None, rows, wo), lambda b, c, pt: (b, 0, 0)),
        scratch_shapes=[pltpu.VMEM((rows, 1), F32), pltpu.VMEM((rows, 1), F32), pltpu.VMEM((rows, lora), F32)])
    return pl.pallas_call(
        functools.partial(_mla_dec_kernel, pps=pps, n_new=n_new, lora=lora, scale=scale),
        grid_spec=grid_spec,
        out_shape=jax.ShapeDtypeStruct((db, rows, wo), F32),
        compiler_params=_cparams(2),
    )(pt, q_rows, lat_new, w_uv_all, *([cache_lat] * pps))


def _sb_dec_kernel(pt_ref, q_ref, kvn_ref, *rest, pps, n_new, hd):
    kv_refs = rest[:pps]
    o_ref, a_ref, acc_ref = rest[pps:]
    c = pl.program_id(1)
    q = q_ref[...]
    rows = q.shape[0]
    msuf = _suffix_matrix(LANES, True)

    @pl.when(c == 0)
    def _():
        valid = _new_token_mask(rows, n_new, True)
        a, acc = _sb_block(_nt(q, kvn_ref[:, :hd].astype(BF16)), kvn_ref[:, hd:].astype(BF16),
                           jnp.zeros((rows, LANES), F32), jnp.zeros((rows, hd), F32), msuf, valid)
        a_ref[...] = a
        acc_ref[...] = acc

    a = a_ref[...]
    acc = acc_ref[...]
    for r in range(pps):
        a, acc = _sb_block(_nt(q, kv_refs[r][:, :hd].astype(BF16)), kv_refs[r][:, hd:].astype(BF16),
                           a, acc, msuf, None)
    a_ref[...] = a
    acc_ref[...] = acc

    @pl.when(c == pl.num_programs(1) - 1)
    def _():
        o_ref[...] = acc


def _sb_dec(q_rows, kv_new, cache_kv, page_table, n_new):
    db, rows, hd = q_rows.shape
    n_pages = page_table.shape[1]
    pps = min(PAGES_PER_STEP, n_pages)
    page, w = cache_kv.shape[1:]
    pt = page_table.reshape(-1)
    grid_spec = pltpu.PrefetchScalarGridSpec(
        num_scalar_prefetch=1,
        grid=(db, n_pages // pps),
        in_specs=[pl.BlockSpec((None, rows, hd), lambda b, c, pt: (b, 0, 0)),
                  pl.BlockSpec((None, LANES, w), lambda b, c, pt: (b, 0, 0))]
        + _page_specs((page, w), n_pages, pps),
        out_specs=pl.BlockSpec((None, rows, hd), lambda b, c, pt: (b, 0, 0)),
        scratch_shapes=[pltpu.VMEM((rows, LANES), F32), pltpu.VMEM((rows, hd), F32)])
    return pl.pallas_call(
        functools.partial(_sb_dec_kernel, pps=pps, n_new=n_new, hd=hd),
        grid_spec=grid_spec,
        out_shape=jax.ShapeDtypeStruct((db, rows, hd), F32),
        compiler_params=_cparams(2),
    )(pt, q_rows, kv_new, *([cache_kv] * pps))


def _layer_norm(v, g, b):
    mu = jnp.mean(v, axis=-1, keepdims=True)
    var = jnp.mean(jnp.square(v - mu), axis=-1, keepdims=True)
    return (v - mu) * lax.rsqrt(var + LN_EPS) * g + b


def _merge_kernel(x_ref, mod_ref, ba_ref, bb_ref, bc_ref, bd_ref, wg_ref, wb_ref, wo_ref, g_ref, b_ref,
                  wrh_ref, wrl_ref, br_ref, x1_ref, h2_ref, ti_ref, tp_ref, *, alpha):
    x = x_ref[...]
    d = x.shape[1]
    h = (x * (1.0 + mod_ref[1]) + mod_ref[0]).astype(BF16)
    y = None
    for n, br in enumerate((ba_ref, bb_ref, bc_ref, bd_ref)):
        gate = jax.nn.sigmoid(_dot(h, wg_ref[:, n * d:(n + 1) * d]))
        term = gate * _dot(br[...], wb_ref[n])
        y = term if y is None else y + term
    y = _dot(y.astype(BF16), wo_ref[...])
    x1 = _layer_norm(alpha * x + (1.0 + mod_ref[2]) * y, g_ref[...], b_ref[...])
    x1_ref[...] = x1
    h2 = x1 * (1.0 + mod_ref[4]) + mod_ref[3]
    h2_ref[...] = h2.astype(BF16)
    hh = h2.astype(BF16)
    hl = (h2 - hh.astype(F32)).astype(BF16)
    logits = _dot(hh, wrh_ref[...]) + _dot(hl, wrh_ref[...]) + _dot(hh, wrl_ref[...]) + br_ref[...]
    ne = logits.shape[1]
    lane = lax.broadcasted_iota(jnp.int32, logits.shape, 1).astype(F32)
    kcol = lax.broadcasted_iota(jnp.int32, (logits.shape[0], TOP_K), 1)
    tv = jnp.zeros((logits.shape[0], TOP_K), F32)
    ti = jnp.zeros((logits.shape[0], TOP_K), F32)
    for k in range(TOP_K):
        m = jnp.max(logits, axis=-1, keepdims=True)
        idx = jnp.min(jnp.where(logits == m, lane, float(ne)), axis=-1, keepdims=True)
        tv = jnp.where(kcol == k, m, tv)
        ti = jnp.where(kcol == k, idx, ti)
        logits = jnp.where(lane == idx, -jnp.inf, logits)
    e = jnp.exp(tv - tv[:, 0:1])
    ti_ref[...] = ti.astype(jnp.int32)
    tp_ref[...] = e / jnp.sum(e, axis=-1, keepdims=True)


def _merge(x2d, mod, branches, w_gate, w_branch, w_out, ln_g, ln_b, wr_hi, wr_lo, b_router, tm, rows_per_seq, alpha):
    n, d = x2d.shape
    full = lambda a: pl.BlockSpec(a.shape, lambda i: (0,) * a.ndim)
    row = lambda w: pl.BlockSpec((tm, w), lambda i: (i, 0))
    return pl.pallas_call(
        functools.partial(_merge_kernel, alpha=alpha),
        grid=(n // tm,),
        in_specs=[row(d), _mod_spec(mod, tm, rows_per_seq)] + [row(b.shape[1]) for b in branches]
        + [full(w_gate), full(w_branch), full(w_out), full(ln_g), full(ln_b), full(wr_hi), full(wr_lo), full(b_router)],
        out_specs=[row(d), row(d), row(TOP_K), row(TOP_K)],
        out_shape=[jax.ShapeDtypeStruct((n, d), F32), jax.ShapeDtypeStruct((n, d), BF16),
                   jax.ShapeDtypeStruct((n, TOP_K), jnp.int32), jax.ShapeDtypeStruct((n, TOP_K), F32)],
        compiler_params=_cparams(1),
    )(x2d, mod, *branches, w_gate, w_branch, w_out, ln_g, ln_b, wr_hi, wr_lo, b_router)


def _expert_kernel(be_ref, nb_ref, x_ref, w1g_ref, w1l_ref, b1g_ref, b1l_ref, w2_ref, b2_ref, o_ref):
    @pl.when(pl.program_id(0) < nb_ref[0])
    def _():
        x = x_ref[...]
        glu = jnp.minimum(_dot(x, w1g_ref[...]) + b1g_ref[...], SWIGLU_LIMIT)
        lin = jnp.clip(_dot(x, w1l_ref[...]) + b1l_ref[...], -SWIGLU_LIMIT, SWIGLU_LIMIT)
        act = glu * jax.nn.sigmoid(SWIGLU_ALPHA * glu) * (lin + 1.0)
        o_ref[...] = _dot(act.astype(BF16), w2_ref[...]) + b2_ref[...]


def _experts(block_exp, n_used, xs, w1g, w1l, b1g, b1l, w2, b2):
    n_slots, d = xs.shape
    ne, _, f = w1g.shape
    nb = n_slots // MOE_ROWS
    blk = lambda i, be, nu: jnp.minimum(i, nu[0] - 1)
    wspec = lambda s: pl.BlockSpec((None,) + s, lambda i, be, nu: (be[jnp.minimum(i, nu[0] - 1)], 0, 0))
    grid_spec = pltpu.PrefetchScalarGridSpec(
        num_scalar_prefetch=2,
        grid=(nb,),
        in_specs=[pl.BlockSpec((MOE_ROWS, d), lambda i, be, nu: (blk(i, be, nu), 0)),
                  wspec((d, f)), wspec((d, f)), wspec((1, f)), wspec((1, f)), wspec((f, d)), wspec((1, d))],
        out_specs=pl.BlockSpec((MOE_ROWS, d), lambda i, be, nu: (blk(i, be, nu), 0)))
    return pl.pallas_call(
        _expert_kernel,
        grid_spec=grid_spec,
        out_shape=jax.ShapeDtypeStruct((n_slots, d), F32),
        compiler_params=_cparams(1),
    )(block_exp, n_used, xs, w1g, w1l, b1g, b1l, w2, b2)


def _final_kernel(x_ref, mod_ref, y_ref, p_ref, g_ref, b_ref, o_ref, *, alpha):
    p = p_ref[...]
    f = None
    for k in range(TOP_K):
        term = p[:, k:k + 1] * y_ref[k]
        f = term if f is None else f + term
    o_ref[...] = _layer_norm(alpha * x_ref[...] + (1.0 + mod_ref[5]) * f, g_ref[...], b_ref[...])


def _final(x1, mod, y4, probs, ln_g, ln_b, tm, rows_per_seq, alpha):
    n, d = x1.shape
    full = lambda a: pl.BlockSpec(a.shape, lambda i: (0,) * a.ndim)
    return pl.pallas_call(
        functools.partial(_final_kernel, alpha=alpha),
        grid=(n // tm,),
        in_specs=[pl.BlockSpec((tm, d), lambda i: (i, 0)), _mod_spec(mod, tm, rows_per_seq),
                  pl.BlockSpec((TOP_K, tm, d), lambda i: (0, i, 0)),
                  pl.BlockSpec((tm, TOP_K), lambda i: (i, 0)), full(ln_g), full(ln_b)],
        out_specs=pl.BlockSpec((tm, d), lambda i: (i, 0)),
        out_shape=jax.ShapeDtypeStruct((n, d), F32),
        compiler_params=_cparams(1),
    )(x1, mod, y4, probs, ln_g, ln_b)


def _rope_tables(pos, rope):
    half = rope // 2
    inv = ROPE_BASE ** (-jnp.arange(half, dtype=F32) / half)
    ang = pos.astype(F32)[:, None] * inv[None, :]
    cos, sin = jnp.cos(ang), jnp.sin(ang)
    reps = LANES // rope
    return (jnp.tile(jnp.concatenate([cos, cos], -1), (1, reps)),
            jnp.tile(jnp.concatenate([-sin, sin], -1), (1, reps)))


def _swap_halves(w):
    half = w.shape[-1] // 2
    return jnp.concatenate([w[..., half:], w[..., :half]], axis=-1)


def _layer_weights(l, p, dims):
    d, dp, hd, q_lora, kv_lora, rope = dims
    w_in = p['w_in'][l]
    widths = (dp, 3 * hd, N_HEADS, q_lora, kv_lora, rope, 3 * hd, N_BRANCH * d)
    offs = [0]
    for w in widths:
        offs.append(offs[-1] + w)
    cols = [w_in[:, offs[i]:offs[i + 1]] for i in range(len(widths))]
    w_pool, w_fox, w_f, w_cq, w_ckv, w_kr, w_sb, w_gate = cols
    small = jnp.concatenate([w_kr, _swap_halves(w_kr), w_f, jnp.zeros((d, LANES - 2 * rope - N_HEADS), F32)], axis=1)
    w_a = jnp.concatenate([w_pool, w_fox, w_cq, w_ckv, w_sb, small], axis=1).astype(BF16)
    eye = jnp.eye(N_HEADS, dtype=F32)
    w_uq = p['w_mla_uq'][l]
    nope = w_uq.shape[-1] - rope
    pad = lambda w: jnp.pad(w, ((0, 0), (0, 0), (0, LANES - rope))).reshape(q_lora, N_HEADS * LANES)
    w_rope = w_uq[:, :, nope:]
    g = dp // len(POOL_WINDOWS)
    lw = {
        'w_a': w_a,
        'w_gate': w_gate.astype(BF16),
        'w_pool_bd': jnp.einsum('gcd,gh->gchd', p['w_pool_mix'][l], jnp.eye(len(POOL_WINDOWS), dtype=F32)
                                ).reshape(dp, dp).astype(BF16),
        'pool_scale': p['pool_scale'][l].reshape(1, dp),
        'fox_bias': jnp.zeros((1, LANES), F32).at[0, 2 * rope:2 * rope + N_HEADS].set(p['b_fox_forget'][l]),
        'g_q': p['mla_q_norm'][l].reshape(1, q_lora),
        'g_kv': p['mla_kv_norm'][l].reshape(1, kv_lora),
        'w_nope': w_uq[:, :, :nope].reshape(q_lora, N_HEADS * nope).astype(BF16),
        'w_rope': pad(w_rope).astype(BF16),
        'w_rope_sw': pad(_swap_halves(w_rope)).astype(BF16),
        'w_uk_bd': jnp.einsum('chn,hg->hngc', p['w_mla_uk'][l], eye).reshape(N_HEADS * nope, N_HEADS * kv_lora).astype(BF16),
        'w_uv_bd': jnp.einsum('chv,hg->hcgv', p['w_mla_uv'][l], eye).reshape(N_HEADS * kv_lora, -1).astype(BF16),
        'w_uv_all': p['w_mla_uv'][l].reshape(kv_lora, -1).astype(BF16),
        'w_branch': p['w_branch'][l].astype(BF16),
        'w_out': p['w_out'][l].astype(BF16),
        'ln1_g': p['ln1_g'][l].reshape(1, d), 'ln1_b': p['ln1_b'][l].reshape(1, d),
        'ln2_g': p['ln2_g'][l].reshape(1, d), 'ln2_b': p['ln2_b'][l].reshape(1, d),
        'b_router': p['b_router'][l].reshape(1, -1),
        'w1g': p['w_exp1'][l][:, :, 0::2].astype(BF16), 'w1l': p['w_exp1'][l][:, :, 1::2].astype(BF16),
        'b1g': p['b_exp1'][l][:, None, 0::2], 'b1l': p['b_exp1'][l][:, None, 1::2],
        'w2': p['w_exp2'][l].astype(BF16), 'b2': p['b_exp2'][l][:, None, :],
    }
    wr = p['w_router'][l]
    lw['wr_hi'] = wr.astype(BF16)
    lw['wr_lo'] = (wr - lw['wr_hi'].astype(F32)).astype(BF16)
    lw['widths'] = (dp, 3 * hd, q_lora + kv_lora, 3 * hd, LANES)
    lw['nope'] = nope
    return lw


def _routing(top_i, n_experts):
    n = top_i.shape[0]
    nk = n * TOP_K
    e_flat = top_i.reshape(nk)
    order = jnp.argsort(e_flat)
    sizes = jnp.bincount(e_flat, length=n_experts)
    padded = (sizes + MOE_ROWS - 1) // MOE_ROWS * MOE_ROWS
    ends = jnp.cumsum(padded)
    e_sorted = e_flat[order]
    dest_sorted = (ends - padded)[e_sorted] + jnp.arange(nk) - (jnp.cumsum(sizes) - sizes)[e_sorted]
    dest = jnp.zeros(nk, jnp.int32).at[order].set(dest_sorted.astype(jnp.int32))
    n_blocks = -(-(nk + n_experts * (MOE_ROWS - 1)) // MOE_ROWS)
    slot_tok = jnp.zeros(n_blocks * MOE_ROWS, jnp.int32).at[dest].set(jnp.arange(nk, dtype=jnp.int32) // TOP_K)
    block_exp = jnp.minimum(jnp.searchsorted(ends, jnp.arange(n_blocks) * MOE_ROWS, side='right'), n_experts - 1)
    n_used = (ends[-1] // MOE_ROWS).astype(jnp.int32).reshape(1)
    return slot_tok, dest, block_exp.astype(jnp.int32), n_used


def _head_rows(q, scale):
    db, dt, hd = q.shape
    dh = hd // N_HEADS
    lane_head = jnp.arange(hd) // dh
    rows = jnp.where(lane_head[None, None, None, :] == jnp.arange(N_HEADS)[None, :, None, None],
                     (q * scale)[:, None, :, :], 0.0)
    return rows.reshape(db, N_HEADS * dt, hd).astype(BF16)


def _diag_heads(o, dt):
    db, _, hv = o.shape
    dv = hv // N_HEADS
    o5 = o.reshape(db, N_HEADS, dt, N_HEADS, dv)
    d = jnp.stack([o5[:, h, :, h, :] for h in range(N_HEADS)], axis=2)
    return d.reshape(db * dt, hv).astype(BF16)


def _pad_rows(a, rows):
    return jnp.pad(a, ((0, 0), (0, rows - a.shape[1]), (0, 0)))


def kernel(x_prompt, x_sample, c_prompt, c_sample, cache_fox_kv, cache_fox_logf, cache_mla, cache_sb_kv, state_pool, page_table, w_ada, b_ada, w_in, b_fox_forget, w_pool_mix, pool_scale, mla_q_norm, w_mla_uq, mla_kv_norm, w_mla_uk, w_mla_uv, w_branch, w_out, ln1_g, ln1_b, w_router, b_router, w_exp1, b_exp1, w_exp2, b_exp2, ln2_g, ln2_b):
    p = dict(w_in=w_in, b_fox_forget=b_fox_forget, w_pool_mix=w_pool_mix, pool_scale=pool_scale,
             mla_q_norm=mla_q_norm, w_mla_uq=w_mla_uq, mla_kv_norm=mla_kv_norm, w_mla_uk=w_mla_uk,
             w_mla_uv=w_mla_uv, w_branch=w_branch, w_out=w_out, ln1_g=ln1_g, ln1_b=ln1_b, w_router=w_router,
             b_router=b_router, w_exp1=w_exp1, b_exp1=b_exp1, w_exp2=w_exp2, b_exp2=b_exp2, ln2_g=ln2_g, ln2_b=ln2_b)
    b, t, d = x_prompt.shape
    db, dt, _ = x_sample.shape
    depth = w_ada.shape[0]
    n_pages = page_table.shape[1]
    page = cache_fox_kv.shape[2]
    past = n_pages * page
    dp = state_pool.shape[-1]
    hd = cache_fox_kv.shape[-1] * cache_fox_kv.shape[-2]
    dh = hd // N_HEADS
    q_lora = mla_q_norm.shape[1]
    kv_lora = mla_kv_norm.shape[1]
    rope = cache_mla.shape[-1] - kv_lora
    n_experts = w_router.shape[-1]
    alpha = (2 * depth) ** 0.25
    dims = (d, dp, hd, q_lora, kv_lora, rope)
    np_rows, ns_rows = b * t, db * dt
    tm = min(ROW_TILE, t)
    tile = min(ATT_TILE, t)
    nq = t // tile

    mods = _ada(jnp.concatenate([c_prompt, c_sample], axis=0), w_ada, b_ada)
    cos_p, sin_p = _rope_tables(jnp.arange(t), rope)
    cos_s, sin_s = (jnp.tile(a, (db, 1)) for a in _rope_tables(past + jnp.arange(dt), rope))

    xp = x_prompt.reshape(np_rows, d)
    xs = x_sample.reshape(ns_rows, d)
    outs = {k: [] for k in ('fox_p', 'fox_s', 'lf_p', 'lf_s', 'mla_p', 'mla_s', 'sb_p', 'sb_s', 'pool_p', 'pool_s')}
    for l in range(depth):
        lw = _layer_weights(l, p, dims)
        mod_p = jnp.transpose(mods[l, :, :b], (1, 0, 2))[:, :, None, :]
        mod_s = jnp.repeat(mods[l, :, b:], dt, axis=1)[None]
        mla_scale = (lw['nope'] + rope) ** -0.5

        u_pool, u_fox, u_mla, u_sb, u_small = _inproj(xp, mod_p, lw['w_a'], lw['widths'], tm, t)
        br_a, pool_new = _pool(u_pool.reshape(b, t, dp), jnp.zeros((b, POOL_BUF, dp), F32),
                               lw['w_pool_bd'], lw['pool_scale'], 0)
        lf, cum = _fox_prep(u_small.reshape(b, t, LANES), lw['fox_bias'], 2 * rope)
        c_s = jnp.transpose(cum, (0, 2, 1)).reshape(b, N_HEADS, nq, 1, tile)
        br_b = _fox_attn(u_fox.reshape(b, t, 3 * hd), cum, c_s, tile)
        q_cat, lat_pad, lat_new = _mla_prep(u_mla, u_small, cos_p, sin_p, lw['g_q'], lw['g_kv'], lw['w_nope'],
                                            lw['w_rope'], lw['w_rope_sw'], lw['w_uk_bd'], tm, rope)
        br_c = _mla_attn(q_cat.reshape(b, t, -1), lat_pad.reshape(b, t, -1), lw['w_uv_bd'], tile, kv_lora, mla_scale)
        br_d = _sb_attn(u_sb.reshape(b, t, 3 * hd), tile)
        branches = [br_a.reshape(np_rows, dp), br_b.reshape(np_rows, hd), br_c.reshape(np_rows, hd),
                    br_d.reshape(np_rows, hd)]
        x1_p, h2_p, ti_p, tp_p = _merge(xp, mod_p, branches, lw['w_gate'], lw['w_branch'], lw['w_out'],
                                        lw['ln1_g'], lw['ln1_b'], lw['wr_hi'], lw['wr_lo'], lw['b_router'],
                                        tm, t, alpha)
        outs['fox_p'].append(u_fox[:, hd:].reshape(b, t, 2, N_HEADS, dh))
        outs['lf_p'].append(lf)
        outs['mla_p'].append(lat_new.reshape(b, t, kv_lora + rope))
        outs['sb_p'].append(u_sb[:, hd:].reshape(b, t, 2, N_HEADS, dh))
        outs['pool_p'].append(pool_new)

        u_pool, u_fox, u_mla, u_sb, u_small = _inproj(xs, mod_s, lw['w_a'], lw['widths'], ns_rows, dt)
        br_a, pool_new = _pool(u_pool.reshape(db, dt, dp), state_pool[l], lw['w_pool_bd'], lw['pool_scale'], past)
        lf, cum = _fox_prep(u_small.reshape(db, dt, LANES), lw['fox_bias'], 2 * rope)
        u_fox3 = u_fox.reshape(db, dt, 3 * hd)
        g_new = jnp.broadcast_to(-jnp.transpose(cum, (0, 2, 1))[:, :, None, :], (db, N_HEADS, dt, dt))
        g_new = jnp.pad(g_new.reshape(db, N_HEADS * dt, dt), ((0, 0), (0, 0), (0, LANES - dt)))
        sfx = _fox_sfx(cache_fox_logf[l].reshape(-1, page * N_HEADS), page)
        o_fox = _fox_dec(_head_rows(u_fox3[:, :, :hd], dh ** -0.5), _pad_rows(u_fox3[:, :, hd:], LANES), g_new,
                         cache_fox_kv[l].reshape(-1, page, 2 * hd), sfx, page_table, dt)
        q_cat, lat_pad, lat_new = _mla_prep(u_mla, u_small, cos_s, sin_s, lw['g_q'], lw['g_kv'], lw['w_nope'],
                                            lw['w_rope'], lw['w_rope_sw'], lw['w_uk_bd'], ns_rows, rope)
        wq = kv_lora + LANES
        q_rows = q_cat.reshape(db, dt, N_HEADS, wq)[..., :kv_lora + rope]
        q_rows = jnp.transpose(q_rows, (0, 2, 1, 3)).reshape(db, N_HEADS * dt, kv_lora + rope)
        o_mla = _mla_dec(q_rows, _pad_rows(lat_new.reshape(db, dt, -1), LANES), lw['w_uv_all'], cache_mla[l],
                         page_table, dt, kv_lora, mla_scale)
        u_sb3 = u_sb.reshape(db, dt, 3 * hd)
        o_sb = _sb_dec(_head_rows(u_sb3[:, :, :hd], dh ** -0.5), _pad_rows(u_sb3[:, :, hd:], LANES),
                       cache_sb_kv[l].reshape(-1, page, 2 * hd), page_table, dt)
        branches = [br_a.reshape(ns_rows, dp), _diag_heads(o_fox, dt), _diag_heads(o_mla, dt), _diag_heads(o_sb, dt)]
        x1_s, h2_s, ti_s, tp_s = _merge(xs, mod_s, branches, lw['w_gate'], lw['w_branch'], lw['w_out'],
                                        lw['ln1_g'], lw['ln1_b'], lw['wr_hi'], lw['wr_lo'], lw['b_router'],
                                        ns_rows, dt, alpha)
        outs['fox_s'].append(u_fox3[:, :, hd:].reshape(db, dt, 2, N_HEADS, dh))
        outs['lf_s'].append(lf)
        outs['mla_s'].append(lat_new.reshape(db, dt, kv_lora + rope))
        outs['sb_s'].append(u_sb3[:, :, hd:].reshape(db, dt, 2, N_HEADS, dh))
        outs['pool_s'].append(pool_new)

        h2 = jnp.concatenate([h2_p, h2_s], axis=0)
        top_i = jnp.concatenate([ti_p, ti_s], axis=0)
        slot_tok, dest, block_exp, n_used = _routing(top_i, n_experts)
        y_slots = _experts(block_exp, n_used, h2[slot_tok], lw['w1g'], lw['w1l'], lw['b1g'], lw['b1l'],
                           lw['w2'], lw['b2'])
        dest_k = dest.reshape(-1, TOP_K).T
        xp = _final(x1_p, mod_p, y_slots[dest_k[:, :np_rows]], tp_p, lw['ln2_g'], lw['ln2_b'], tm, t, alpha)
        xs = _final(x1_s, mod_s, y_slots[dest_k[:, np_rows:]], tp_s, lw['ln2_g'], lw['ln2_b'], ns_rows, dt, alpha)

    st = lambda k: jnp.stack(outs[k])
    return (xp.reshape(b, t, d), xs.reshape(db, dt, d), st('fox_p'), st('fox_s'), st('lf_p'), st('lf_s'),
            st('mla_p'), st('mla_s'), st('sb_p'), st('sb_s'), st('pool_p'), st('pool_s'))
```

```python
import functools

import jax
import jax.numpy as jnp
from jax import lax
from jax.experimental import pallas as pl
from jax.experimental.pallas import tpu as pltpu

F32 = jnp.float32
BF16 = jnp.bfloat16

POOL_WINDOWS = (2, 4, 8, 16)
POOL_BUF = max(POOL_WINDOWS) - 1
N_HEADS = 4
N_BRANCH = 4
N_MOD = 6
TOP_K = 4
ROPE_BASE = 10000.0
SWIGLU_ALPHA = 1.702
SWIGLU_LIMIT = 7.0
LN_EPS = 1e-5
RMS_EPS = 1e-6

LANES = 128
SUBLANES = 8
VMEM_LIMIT = 56 * 1024 * 1024

NEG = -1e30

ROW_TILE = 512
ATT_TILE = 512
SB_TILE = 256
PAGES_PER_STEP = 16
MOE_ROWS = 256
DEC_ROWS = 16


def _cparams(n_axes):
    return pltpu.CompilerParams(dimension_semantics=("arbitrary",) * n_axes, vmem_limit_bytes=VMEM_LIMIT)


def _nt(a, b):
    return lax.dot_general(a, b, (((1,), (1,)), ((), ())), preferred_element_type=F32)


def _dot(a, b):
    return jnp.dot(a, b, preferred_element_type=F32)


def _split3(x):
    hi = x.astype(BF16)
    r = x - hi.astype(F32)
    mid = r.astype(BF16)
    lo = (r - mid.astype(F32)).astype(BF16)
    return hi, mid, lo


def _dot_f32_lhs(x, m01):
    hi, mid, lo = _split3(x)
    return _dot(hi, m01) + _dot(mid, m01) + _dot(lo, m01)


def _dot_f32_lhs2(x, m01):
    hi = x.astype(BF16)
    lo = (x - hi.astype(F32)).astype(BF16)
    return _dot(hi, m01) + _dot(lo, m01)


def _dot_f32_rhs(m01, x):
    hi, mid, lo = _split3(x)
    return _dot(m01, hi) + _dot(m01, mid) + _dot(m01, lo)


def _log_sigmoid(x):
    return jnp.minimum(x, 0.0) - jnp.log1p(jnp.exp(-jnp.abs(x)))


def _mod_spec(mod, tm, rows_per_seq):
    d = mod.shape[-1]
    if mod.shape[2] == 1:
        return pl.BlockSpec((None, N_MOD, 1, d), lambda i: ((i * tm) // rows_per_seq, 0, 0, 0))
    return pl.BlockSpec((None, N_MOD, tm, d), lambda i: (0, 0, i, 0))


def _ada_kernel(c_ref, w_ref, b_ref, o_ref):
    c = c_ref[...]
    s = c * jax.nn.sigmoid(c)
    o_ref[...] = _dot(s.astype(BF16), w_ref[...].astype(BF16)) + b_ref[...]


def _ada(c_all, w_ada, b_ada):
    depth, d, _ = w_ada.shape
    r = c_all.shape[0]
    b4 = b_ada.reshape(depth, N_MOD, 1, d)
    return pl.pallas_call(
        _ada_kernel,
        grid=(depth, N_MOD),
        in_specs=[pl.BlockSpec((r, d), lambda l, k: (0, 0)),
                  pl.BlockSpec((None, d, d), lambda l, k: (l, 0, k)),
                  pl.BlockSpec((None, None, 1, d), lambda l, k: (l, k, 0, 0))],
        out_specs=pl.BlockSpec((None, None, r, d), lambda l, k: (l, k, 0, 0)),
        out_shape=jax.ShapeDtypeStruct((depth, N_MOD, r, d), F32),
        compiler_params=_cparams(2),
    )(c_all, w_ada, b4)


def _inproj_kernel(x_ref, mod_ref, w_ref, *out_refs, widths):
    h = x_ref[...] * (1.0 + mod_ref[1]) + mod_ref[0]
    u = _dot(h.astype(BF16), w_ref[...])
    off = 0
    for o_ref, w in zip(out_refs, widths):
        o_ref[...] = u[:, off:off + w]
        off += w


def _inproj(x2d, mod, w_a, widths, tm, rows_per_seq):
    n, d = x2d.shape
    return pl.pallas_call(
        functools.partial(_inproj_kernel, widths=widths),
        grid=(n // tm,),
        in_specs=[pl.BlockSpec((tm, d), lambda i: (i, 0)),
                  _mod_spec(mod, tm, rows_per_seq),
                  pl.BlockSpec(w_a.shape, lambda i: (0, 0))],
        out_specs=[pl.BlockSpec((tm, w), lambda i: (i, 0)) for w in widths],
        out_shape=[jax.ShapeDtypeStruct((n, w), F32) for w in widths],
        compiler_params=_cparams(1),
    )(x2d, mod, w_a)


def _pool_kernel(u_ref, pre_ref, w_ref, sc_ref, o_ref, new_ref, z_ref, *, t, pos0):
    total, dp = z_ref.shape
    base = POOL_BUF + 1
    z_ref[0:1, :] = jnp.zeros((1, dp), F32)
    z_ref[1:base, :] = pre_ref[...]
    z_ref[base:base + t, :] = u_ref[...]
    if total > base + t:
        z_ref[base + t:total, :] = jnp.zeros((total - base - t, dp), F32)
    z = z_ref[...]
    sums = []
    s = z
    for step in (1, 2, 4, 8):
        s = s + pltpu.roll(s, step, 0)
        sums.append(s)
    lane = lax.broadcasted_iota(jnp.int32, (t, dp), 1)
    group = dp // len(POOL_WINDOWS)
    win = sums[-1][base:base + t]
    wlen = jnp.full((t, dp), float(POOL_WINDOWS[-1]), F32)
    for g in range(len(POOL_WINDOWS) - 2, -1, -1):
        sel = lane < (g + 1) * group
        win = jnp.where(sel, sums[g][base:base + t], win)
        wlen = jnp.where(sel, float(POOL_WINDOWS[g]), wlen)
    pos = (pos0 + lax.broadcasted_iota(jnp.int32, (t, dp), 0)).astype(F32)
    cnt = jnp.minimum(wlen, pos + 1.0)
    pooled = win / cnt - u_ref[...]
    o_ref[...] = (_dot(pooled.astype(BF16), w_ref[...]) * sc_ref[...]).astype(o_ref.dtype)
    new_ref[...] = z_ref[t + 1:t + 1 + POOL_BUF, :]


def _pool(u_pool, prefix, w_bd, scale, pos0):
    b, t, dp = u_pool.shape
    total = -(-(POOL_BUF + 1 + t) // SUBLANES) * SUBLANES
    return pl.pallas_call(
        functools.partial(_pool_kernel, t=t, pos0=pos0),
        grid=(b,),
        in_specs=[pl.BlockSpec((None, t, dp), lambda i: (i, 0, 0)),
                  pl.BlockSpec((None, POOL_BUF, dp), lambda i: (i, 0, 0)),
                  pl.BlockSpec((dp, dp), lambda i: (0, 0)),
                  pl.BlockSpec((1, dp), lambda i: (0, 0))],
        out_specs=[pl.BlockSpec((None, t, dp), lambda i: (i, 0, 0)),
                   pl.BlockSpec((None, POOL_BUF, dp), lambda i: (i, 0, 0))],
        out_shape=[jax.ShapeDtypeStruct((b, t, dp), BF16),
                   jax.ShapeDtypeStruct((b, POOL_BUF, dp), F32)],
        scratch_shapes=[pltpu.VMEM((total, dp), F32)],
        compiler_params=_cparams(1),
    )(u_pool, prefix, w_bd, scale)


def _fox_prep_kernel(s_ref, b_ref, lf_ref, c_ref, *, t, col):
    x = s_ref[...] + b_ref[...]
    lf = _log_sigmoid(x)
    lf_ref[...] = lf[:, col:col + N_HEADS]
    if t % LANES == 0:
        r = lax.broadcasted_iota(jnp.int32, (LANES, LANES), 0)
        c = lax.broadcasted_iota(jnp.int32, (LANES, LANES), 1)
        tri = jnp.where(r >= c, 1.0, 0.0).astype(BF16)
        carry = jnp.zeros((1, LANES), F32)
        for ch in range(t // LANES):
            cs = _dot_f32_rhs(tri, lf[ch * LANES:(ch + 1) * LANES]) + carry
            c_ref[ch * LANES:(ch + 1) * LANES, :] = cs[:, col:col + N_HEADS]
            carry = cs[LANES - 1:LANES, :]
    else:
        acc = jnp.zeros((1, LANES), F32)
        for i in range(t):
            acc = acc + lf[i:i + 1]
            c_ref[i:i + 1, :] = acc[:, col:col + N_HEADS]


def _fox_prep(small, bias_row, col):
    b, t, w = small.shape
    return pl.pallas_call(
        functools.partial(_fox_prep_kernel, t=t, col=col),
        grid=(b,),
        in_specs=[pl.BlockSpec((None, t, w), lambda i: (i, 0, 0)),
                  pl.BlockSpec((1, w), lambda i: (0, 0))],
        out_specs=[pl.BlockSpec((None, t, N_HEADS), lambda i: (i, 0, 0))] * 2,
        out_shape=[jax.ShapeDtypeStruct((b, t, N_HEADS), F32)] * 2,
        compiler_params=_cparams(1),
    )(small, bias_row)


def _softmax_step(carry, s, v):
    m, l, acc = carry
    m_new = jnp.maximum(m, jnp.max(s, axis=-1, keepdims=True))
    p = jnp.exp(s - m_new)
    alpha = jnp.exp(m - m_new)
    l = alpha * l + jnp.sum(p, axis=-1, keepdims=True)
    acc = alpha * acc + _dot(p.astype(BF16), v)
    return m_new, l, acc


def _causal_mask(tq, tk, strict):
    r = lax.broadcasted_iota(jnp.int32, (tq, tk), 0)
    c = lax.broadcasted_iota(jnp.int32, (tq, tk), 1)
    return (c < r) if strict else (c <= r)


def _fox_attn_kernel(q_ref, k_ref, v_ref, ct_ref, cs_ref, o_ref, *, tile, dh):
    i = pl.program_id(1)
    scale = dh ** -0.5
    mask = _causal_mask(tile, tile, False)
    heads = [slice(h * dh, (h + 1) * dh) for h in range(N_HEADS)]
    qs = [(q_ref[:, sl] * scale).astype(BF16) for sl in heads]
    cts = [ct_ref[:, h:h + 1] for h in range(N_HEADS)]

    def step(j, carry, masked):
        rows = pl.ds(pl.multiple_of(j * tile, tile), tile)
        out = []
        for h, sl in enumerate(heads):
            s = _nt(qs[h], k_ref[rows, sl].astype(BF16)) + (cts[h] - cs_ref[h, j])
            if masked:
                s = jnp.where(mask, s, NEG)
            out.append(_softmax_step(carry[h], s, v_ref[rows, sl].astype(BF16)))
        return tuple(out)

    init = tuple((jnp.full((tile, 1), NEG, F32), jnp.zeros((tile, 1), F32), jnp.zeros((tile, dh), F32))
                 for _ in heads)
    carry = lax.fori_loop(0, i, lambda j, c: step(j, c, False), init)
    carry = step(i, carry, True)
    for sl, (m, l, acc) in zip(heads, carry):
        o_ref[:, sl] = (acc / l).astype(o_ref.dtype)


def _fox_attn(u_fox, c_t, c_s, tile):
    b, t, w = u_fox.shape
    hd = w // 3
    dh = hd // N_HEADS
    nq = t // tile
    return pl.pallas_call(
        functools.partial(_fox_attn_kernel, tile=tile, dh=dh),
        grid=(b, nq),
        in_specs=[pl.BlockSpec((None, tile, hd), lambda bi, i: (bi, i, 0)),
                  pl.BlockSpec((None, t, hd), lambda bi, i: (bi, 0, 1)),
                  pl.BlockSpec((None, t, hd), lambda bi, i: (bi, 0, 2)),
                  pl.BlockSpec((None, tile, N_HEADS), lambda bi, i: (bi, i, 0)),
                  pl.BlockSpec((None, N_HEADS, nq, 1, tile), lambda bi, i: (bi, 0, 0, 0, 0))],
        out_specs=pl.BlockSpec((None, tile, hd), lambda bi, i: (bi, i, 0)),
        out_shape=jax.ShapeDtypeStruct((b, t, hd), BF16),
        compiler_params=_cparams(2),
    )(u_fox, u_fox, u_fox, c_t, c_s)


def _mla_attn_kernel(q_ref, lat_ref, wuv_ref, o_ref, *, tile, lora, scale):
    i = pl.program_id(1)
    wq = q_ref.shape[1] // N_HEADS
    q = jnp.concatenate([q_ref[:, h * wq:(h + 1) * wq] for h in range(N_HEADS)], axis=0)
    rows_all = N_HEADS * tile
    r = lax.broadcasted_iota(jnp.int32, (rows_all, tile), 0) % tile
    c = lax.broadcasted_iota(jnp.int32, (rows_all, tile), 1)
    mask = c <= r

    def step(j, carry, masked):
        lat = lat_ref[pl.ds(pl.multiple_of(j * tile, tile), tile), :]
        s = _nt(q, lat) * scale
        if masked:
            s = jnp.where(mask, s, NEG)
        return _softmax_step(carry, s, lat[:, :lora])

    init = (jnp.full((rows_all, 1), NEG, F32), jnp.zeros((rows_all, 1), F32), jnp.zeros((rows_all, lora), F32))
    carry = lax.fori_loop(0, i, lambda j, c: step(j, c, False), init)
    m, l, acc = step(i, carry, True)
    o = (acc / l).astype(BF16)
    o = jnp.concatenate([o[h * tile:(h + 1) * tile] for h in range(N_HEADS)], axis=-1)
    o_ref[...] = _dot(o, wuv_ref[...]).astype(o_ref.dtype)


def _mla_attn(q_cat, lat_pad, w_uv_bd, tile, lora, scale):
    b, t, wq = q_cat.shape
    wl = lat_pad.shape[-1]
    wo = w_uv_bd.shape[1]
    return pl.pallas_call(
        functools.partial(_mla_attn_kernel, tile=tile, lora=lora, scale=scale),
        grid=(b, t // tile),
        in_specs=[pl.BlockSpec((None, tile, wq), lambda bi, i: (bi, i, 0)),
                  pl.BlockSpec((None, t, wl), lambda bi, i: (bi, 0, 0)),
                  pl.BlockSpec(w_uv_bd.shape, lambda bi, i: (0, 0))],
        out_specs=pl.BlockSpec((None, tile, wo), lambda bi, i: (bi, i, 0)),
        out_shape=jax.ShapeDtypeStruct((b, t, wo), BF16),
        compiler_params=_cparams(2),
    )(q_cat, lat_pad, w_uv_bd)


def _suffix_matrix(n, with_total):
    j = lax.broadcasted_iota(jnp.int32, (n, n), 0)
    s = lax.broadcasted_iota(jnp.int32, (n, n), 1)
    m = jnp.where(j > s, 1.0, 0.0).astype(BF16)
    if with_total:
        m = jnp.concatenate([m, jnp.ones((n, n), BF16)], axis=1)
    return m


def _sb_weights(z, carry_a, msuf, valid):
    n = z.shape[1]
    ls = _log_sigmoid(z)
    lfail = ls - z
    if valid is not None:
        lfail = jnp.where(valid, lfail, 0.0)
    ct = _dot_f32_lhs2(lfail, msuf)
    w = jnp.exp(ls + ct[:, :n] + carry_a)
    if valid is not None:
        w = jnp.where(valid, w, 0.0)
    return w, carry_a + ct[:, n:]


def _sb_block(z, v, carry_a, acc, msuf, valid):
    w, carry_a = _sb_weights(z, carry_a, msuf, valid)
    return carry_a, acc + _dot(w.astype(BF16), v)


def _sb_attn_kernel(q_ref, k_ref, v_ref, o_ref, *, tile, dh):
    i = pl.program_id(1)
    scale = dh ** -0.5
    valid = _causal_mask(tile, tile, True)
    msuf = _suffix_matrix(tile, True)
    heads = [slice(h * dh, (h + 1) * dh) for h in range(N_HEADS)]
    qs = [(q_ref[:, sl] * scale).astype(BF16) for sl in heads]

    def step(j, carry, mask):
        rows = pl.ds(pl.multiple_of(j * tile, tile), tile)
        return tuple(_sb_block(_nt(qs[h], k_ref[rows, sl].astype(BF16)), v_ref[rows, sl].astype(BF16),
                               carry[h][0], carry[h][1], msuf, mask) for h, sl in enumerate(heads))

    init = tuple((jnp.zeros((tile, tile), F32), jnp.zeros((tile, dh), F32)) for _ in heads)
    carry = step(i, init, valid)
    carry = lax.fori_loop(0, i, lambda t, c: step(i - 1 - t, c, None), carry)
    for sl, (a, acc) in zip(heads, carry):
        o_ref[:, sl] = acc.astype(o_ref.dtype)


def _sb_attn(u_sb, tile):
    b, t, w = u_sb.shape
    hd = w // 3
    dh = hd // N_HEADS
    return pl.pallas_call(
        functools.partial(_sb_attn_kernel, tile=tile, dh=dh),
        grid=(b, t // tile),
        in_specs=[pl.BlockSpec((None, tile, hd), lambda bi, i: (bi, i, 0)),
                  pl.BlockSpec((None, t, hd), lambda bi, i: (bi, 0, 1)),
                  pl.BlockSpec((None, t, hd), lambda bi, i: (bi, 0, 2))],
        out_specs=pl.BlockSpec((None, tile, hd), lambda bi, i: (bi, i, 0)),
        out_shape=jax.ShapeDtypeStruct((b, t, hd), BF16),
        compiler_params=_cparams(2),
    )(u_sb, u_sb, u_sb)


def _rms(x, g):
    return x * lax.rsqrt(jnp.mean(jnp.square(x), axis=-1, keepdims=True) + RMS_EPS) * g


def _mla_prep_kernel(u_ref, s_ref, cos_ref, sin_ref, gq_ref, gkv_ref, wn_ref, wr_ref, wrs_ref, wuk_ref,
                     qcat_ref, latpad_ref, lat_ref, *, q_lora, kv_lora, rope):
    cos = cos_ref[...]
    sin = sin_ref[...]
    cq = _rms(u_ref[:, :q_lora], gq_ref[...]).astype(BF16)
    q_nope = _dot(cq, wn_ref[...])
    q_rope = _dot(cq, wr_ref[...])
    q_rope_sw = _dot(cq, wrs_ref[...])
    q_abs = _dot(q_nope.astype(BF16), wuk_ref[...])
    pieces = []
    for h in range(N_HEADS):
        blk = slice(h * LANES, (h + 1) * LANES)
        pieces.append(q_abs[:, h * kv_lora:(h + 1) * kv_lora])
        pieces.append(q_rope[:, blk] * cos + q_rope_sw[:, blk] * sin)
    qcat_ref[...] = jnp.concatenate(pieces, axis=-1).astype(qcat_ref.dtype)
    ckv = _rms(u_ref[:, q_lora:q_lora + kv_lora], gkv_ref[...])
    small = s_ref[...]
    kr = small * cos + pltpu.roll(small, LANES - rope, 1) * sin
    lane = lax.broadcasted_iota(jnp.int32, kr.shape, 1)
    kr = jnp.where(lane < rope, kr, 0.0)
    latpad_ref[...] = jnp.concatenate([ckv, kr], axis=-1).astype(latpad_ref.dtype)
    lat_ref[...] = jnp.concatenate([ckv, kr[:, :rope]], axis=-1)


def _mla_prep(u_mla, small, cos, sin, g_q, g_kv, w_nope, w_rope, w_rope_sw, w_uk_bd, tm, rope):
    n, wm = u_mla.shape
    q_lora = g_q.shape[1]
    kv_lora = g_kv.shape[1]
    ncs = cos.shape[0] // tm
    wq = N_HEADS * (kv_lora + LANES)
    full = lambda a: pl.BlockSpec(a.shape, lambda i: (0, 0))
    return pl.pallas_call(
        functools.partial(_mla_prep_kernel, q_lora=q_lora, kv_lora=kv_lora, rope=rope),
        grid=(n // tm,),
        in_specs=[pl.BlockSpec((tm, wm), lambda i: (i, 0)),
                  pl.BlockSpec((tm, LANES), lambda i: (i, 0)),
                  pl.BlockSpec((tm, LANES), lambda i: (i % ncs, 0)),
                  pl.BlockSpec((tm, LANES), lambda i: (i % ncs, 0)),
                  full(g_q), full(g_kv), full(w_nope), full(w_rope), full(w_rope_sw), full(w_uk_bd)],
        out_specs=[pl.BlockSpec((tm, wq), lambda i: (i, 0)),
                   pl.BlockSpec((tm, kv_lora + LANES), lambda i: (i, 0)),
                   pl.BlockSpec((tm, kv_lora + rope), lambda i: (i, 0))],
        out_shape=[jax.ShapeDtypeStruct((n, wq), BF16),
                   jax.ShapeDtypeStruct((n, kv_lora + LANES), BF16),
                   jax.ShapeDtypeStruct((n, kv_lora + rope), F32)],
        compiler_params=_cparams(1),
    )(u_mla, small, cos, sin, g_q, g_kv, w_nope, w_rope, w_rope_sw, w_uk_bd)


def _page_specs(block, layer, n_pages, pps):
    zeros = (0,) * len(block)

    def spec(r):
        return pl.BlockSpec((None, None) + block,
                            lambda b, c, pt, r=r: (layer, pt[b * n_pages + (n_pages - 1 - (c * pps + r))]) + zeros)
    return [spec(r) for r in range(pps)]


def _new_token_mask(rows, n_new, strict):
    r = lax.broadcasted_iota(jnp.int32, (rows, LANES), 0) % (rows // N_HEADS)
    c = lax.broadcasted_iota(jnp.int32, (rows, LANES), 1)
    limit = jnp.minimum(r if strict else r + 1, n_new)
    return c < limit


def _head_scores(q, kv_refs):
    qr = q.shape[0] // N_HEADS
    ss, vs = [], []
    for h in range(N_HEADS):
        kt = jnp.concatenate([ref[0, h].astype(BF16) for ref in kv_refs], axis=1)
        ss.append(_dot(q[h * qr:(h + 1) * qr], kt))
        vs.append(jnp.concatenate([ref[1, h].astype(BF16) for ref in kv_refs], axis=1))
    return jnp.concatenate(ss, axis=0), vs


def _head_pv(p, vs):
    qr = p.shape[0] // N_HEADS
    return jnp.concatenate([_nt(p[h * qr:(h + 1) * qr].astype(BF16), vs[h]) for h in range(N_HEADS)], axis=0)


def _fox_sfx_kernel(x_ref, m_ref, o_ref):
    o_ref[...] = _dot_f32_lhs(x_ref[...], m_ref[...])


def _fox_sfx(logf_rows):
    n, page = logf_rows.shape
    tr = 2048 if n % 2048 == 0 else n
    m = _suffix_matrix(page, True)
    return pl.pallas_call(
        _fox_sfx_kernel,
        grid=(n // tr,),
        in_specs=[pl.BlockSpec((tr, page), lambda i: (i, 0)), pl.BlockSpec(m.shape, lambda i: (0, 0))],
        out_specs=pl.BlockSpec((tr, 2 * page), lambda i: (i, 0)),
        out_shape=jax.ShapeDtypeStruct((n, 2 * page), F32),
        compiler_params=_cparams(1),
    )(logf_rows, m)


def _softmax_update(m_ref, l_ref, acc_ref, s, pv, first):
    smax = jnp.max(s, axis=-1, keepdims=True)
    if first:
        m_new = smax
    else:
        m_old = m_ref[...]
        m_new = jnp.maximum(m_old, smax)
        alpha = jnp.exp(m_old - m_new)
    p = jnp.exp(s - m_new)
    l = jnp.sum(p, axis=-1, keepdims=True)
    acc = pv(p)
    if not first:
        l = l + alpha * l_ref[...]
        acc = acc + alpha * acc_ref[...]
    m_ref[...] = m_new
    l_ref[...] = l
    acc_ref[...] = acc
    return l, acc


def _fox_dec_kernel(pt_ref, q_ref, kvn_ref, gn_ref, *rest, pps, n_new):
    kv_refs, sfx_refs = rest[:pps], rest[pps:2 * pps]
    o_ref, m_ref, l_ref, acc_ref, car_ref = rest[2 * pps:]
    c = pl.program_id(1)
    q = q_ref[...]
    rows = q.shape[0]
    qr = rows // N_HEADS

    @pl.when(c == 0)
    def _():
        s, vs = _head_scores(q, [kvn_ref])
        s = jnp.where(_new_token_mask(rows, n_new, False), s + gn_ref[...], NEG)
        _softmax_update(m_ref, l_ref, acc_ref, s, lambda p: _head_pv(p, vs), True)
        car_ref[...] = jnp.zeros(car_ref.shape, F32)

    car = car_ref[0:N_HEADS, :]
    biases = []
    for r in range(pps):
        blk = sfx_refs[r][...]
        biases.append(blk[:, :LANES] + car)
        car = car + blk[:, LANES:]
    car_ref[0:N_HEADS, :] = car
    bias = jnp.concatenate(biases, axis=1)
    s, vs = _head_scores(q, kv_refs)
    s = s + jnp.concatenate([jnp.broadcast_to(bias[h:h + 1, :], (qr, bias.shape[1])) for h in range(N_HEADS)], axis=0)
    l, acc = _softmax_update(m_ref, l_ref, acc_ref, s, lambda p: _head_pv(p, vs), False)

    @pl.when(c == pl.num_programs(1) - 1)
    def _():
        o_ref[...] = acc / l


def _fox_dec(layer, q_rows, kv_new, g_new, cache_kv, sfx, page_table, n_new):
    db, rows, dh = q_rows.shape
    n_pages = page_table.shape[1]
    pps = min(PAGES_PER_STEP, n_pages)
    kv_block = cache_kv.shape[2:]
    sfx_block = sfx.shape[2:]
    grid_spec = pltpu.PrefetchScalarGridSpec(
        num_scalar_prefetch=1,
        grid=(db, n_pages // pps),
        in_specs=[pl.BlockSpec((None, rows, dh), lambda b, c, pt: (b, 0, 0)),
                  pl.BlockSpec((None,) + kv_block, lambda b, c, pt: (b, 0, 0, 0, 0)),
                  pl.BlockSpec((None, rows, LANES), lambda b, c, pt: (b, 0, 0))]
        + _page_specs(kv_block, layer, n_pages, pps) + _page_specs(sfx_block, layer, n_pages, pps),
        out_specs=pl.BlockSpec((None, rows, dh), lambda b, c, pt: (b, 0, 0)),
        scratch_shapes=[pltpu.VMEM((rows, 1), F32), pltpu.VMEM((rows, 1), F32), pltpu.VMEM((rows, dh), F32),
                        pltpu.VMEM((SUBLANES, LANES), F32)])
    return pl.pallas_call(
        functools.partial(_fox_dec_kernel, pps=pps, n_new=n_new),
        grid_spec=grid_spec,
        out_shape=jax.ShapeDtypeStruct((db, rows, dh), F32),
        compiler_params=_cparams(2),
    )(page_table.reshape(-1), q_rows, kv_new, g_new, *([cache_kv] * pps), *([sfx] * pps))


def _mla_dec_kernel(pt_ref, q_ref, latn_ref, wuv_ref, *rest, pps, n_new, lora, scale):
    lat_refs = rest[:pps]
    o_ref, m_ref, l_ref, acc_ref = rest[pps:]
    c = pl.program_id(1)
    q = q_ref[...]
    rows = q.shape[0]

    @pl.when(c == 0)
    def _():
        lat = latn_ref[...].astype(BF16)
        s = jnp.where(_new_token_mask(rows, n_new, False), _dot(q, lat) * scale, NEG)
        _softmax_update(m_ref, l_ref, acc_ref, s, lambda p: _nt(p.astype(BF16), lat[:lora]), True)

    lat = jnp.concatenate([lat_refs[r][...].astype(BF16) for r in range(pps)], axis=1)
    l, acc = _softmax_update(m_ref, l_ref, acc_ref, _dot(q, lat) * scale,
                             lambda p: _nt(p.astype(BF16), lat[:lora]), False)

    @pl.when(c == pl.num_programs(1) - 1)
    def _():
        o_ref[...] = _dot((acc / l).astype(BF16), wuv_ref[...])


def _mla_dec(layer, q_rows, lat_new, w_uv_all, cache_lat, page_table, n_new, lora, scale):
    db, rows, wq = q_rows.shape
    n_pages = page_table.shape[1]
    pps = min(PAGES_PER_STEP, n_pages)
    lat_block = cache_lat.shape[2:]
    wo = w_uv_all.shape[1]
    grid_spec = pltpu.PrefetchScalarGridSpec(
        num_scalar_prefetch=1,
        grid=(db, n_pages // pps),
        in_specs=[pl.BlockSpec((None, rows, wq), lambda b, c, pt: (b, 0, 0)),
                  pl.BlockSpec((None,) + lat_block, lambda b, c, pt: (b, 0, 0)),
                  pl.BlockSpec(w_uv_all.shape, lambda b, c, pt: (0, 0))]
        + _page_specs(lat_block, layer, n_pages, pps),
        out_specs=pl.BlockSpec((None, rows, wo), lambda b, c, pt: (b, 0, 0)),
        scratch_shapes=[pltpu.VMEM((rows, 1), F32), pltpu.VMEM((rows, 1), F32), pltpu.VMEM((rows, lora), F32)])
    return pl.pallas_call(
        functools.partial(_mla_dec_kernel, pps=pps, n_new=n_new, lora=lora, scale=scale),
        grid_spec=grid_spec,
        out_shape=jax.ShapeDtypeStruct((db, rows, wo), F32),
        compiler_params=_cparams(2),
    )(page_table.reshape(-1), q_rows, lat_new, w_uv_all, *([cache_lat] * pps))


def _sb_dec_kernel(pt_ref, q_ref, kvn_ref, *rest, pps, n_new):
    kv_refs = rest[:pps]
    o_ref, a_ref, acc_ref = rest[pps:]
    c = pl.program_id(1)
    q = q_ref[...]
    rows = q.shape[0]
    msuf = _suffix_matrix(LANES, True)

    @pl.when(c == 0)
    def _():
        z, vs = _head_scores(q, [kvn_ref])
        w, a = _sb_weights(z, jnp.zeros(a_ref.shape, F32), msuf, _new_token_mask(rows, n_new, True))
        a_ref[...] = a
        acc_ref[...] = _head_pv(w, vs)

    z, vs = _head_scores(q, kv_refs)
    ls = _log_sigmoid(z)
    lfail = ls - z
    stacked = jnp.concatenate([lfail[:, r * LANES:(r + 1) * LANES] for r in range(pps)], axis=0)
    ct = _dot_f32_lhs2(stacked, msuf)
    a = a_ref[...]
    cums = []
    for r in range(pps):
        ctr = ct[r * rows:(r + 1) * rows]
        cums.append(ctr[:, :LANES] + a)
        a = a + ctr[:, LANES:]
    a_ref[...] = a
    acc = acc_ref[...] + _head_pv(jnp.exp(ls + jnp.concatenate(cums, axis=1)), vs)
    acc_ref[...] = acc

    @pl.when(c == pl.num_programs(1) - 1)
    def _():
        o_ref[...] = acc


def _sb_dec(layer, q_rows, kv_new, cache_kv, page_table, n_new):
    db, rows, dh = q_rows.shape
    n_pages = page_table.shape[1]
    pps = min(PAGES_PER_STEP, n_pages)
    kv_block = cache_kv.shape[2:]
    grid_spec = pltpu.PrefetchScalarGridSpec(
        num_scalar_prefetch=1,
        grid=(db, n_pages // pps),
        in_specs=[pl.BlockSpec((None, rows, dh), lambda b, c, pt: (b, 0, 0)),
                  pl.BlockSpec((None,) + kv_block, lambda b, c, pt: (b, 0, 0, 0, 0))]
        + _page_specs(kv_block, layer, n_pages, pps),
        out_specs=pl.BlockSpec((None, rows, dh), lambda b, c, pt: (b, 0, 0)),
        scratch_shapes=[pltpu.VMEM((rows, LANES), F32), pltpu.VMEM((rows, dh), F32)])
    return pl.pallas_call(
        functools.partial(_sb_dec_kernel, pps=pps, n_new=n_new),
        grid_spec=grid_spec,
        out_shape=jax.ShapeDtypeStruct((db, rows, dh), F32),
        compiler_params=_cparams(2),
    )(page_table.reshape(-1), q_rows, kv_new, *([cache_kv] * pps))


def _layer_norm(v, g, b):
    mu = jnp.mean(v, axis=-1, keepdims=True)
    var = jnp.mean(jnp.square(v - mu), axis=-1, keepdims=True)
    return (v - mu) * lax.rsqrt(var + LN_EPS) * g + b


def _merge_kernel(x_ref, mod_ref, ba_ref, bb_ref, bc_ref, bd_ref, wg_ref, wb_ref, wo_ref, g_ref, b_ref,
                  wr_ref, br_ref, cnt0_ref, x1_ref, h2_ref, ti_ref, tp_ref, rk_ref, cnt_ref, *, alpha):
    @pl.when(pl.program_id(0) == 0)
    def _():
        cnt_ref[...] = cnt0_ref[...]

    x = x_ref[...]
    tm, d = x.shape
    h = (x * (1.0 + mod_ref[1]) + mod_ref[0]).astype(BF16)
    y = None
    for n, br in enumerate((ba_ref, bb_ref, bc_ref, bd_ref)):
        gate = jax.nn.sigmoid(_dot(h, wg_ref[:, n * d:(n + 1) * d]))
        term = gate * _dot(br[...], wb_ref[n])
        y = term if y is None else y + term
    y = _dot(y.astype(BF16), wo_ref[...])
    x1 = _layer_norm(alpha * x + (1.0 + mod_ref[2]) * y, g_ref[...], b_ref[...])
    x1_ref[...] = x1
    h2 = (x1 * (1.0 + mod_ref[4]) + mod_ref[3]).astype(BF16)
    h2_ref[...] = h2
    logits = _dot(h2, wr_ref[...]) + br_ref[...]
    ne = logits.shape[1]
    lane = lax.broadcasted_iota(jnp.int32, logits.shape, 1).astype(F32)
    kcol = lax.broadcasted_iota(jnp.int32, (tm, TOP_K), 1)
    tv = jnp.zeros((tm, TOP_K), F32)
    ti = jnp.zeros((tm, TOP_K), F32)
    picks = []
    for k in range(TOP_K):
        m = jnp.max(logits, axis=-1, keepdims=True)
        idx = jnp.min(jnp.where(logits == m, lane, float(ne)), axis=-1, keepdims=True)
        pick = lane == idx
        picks.append(pick)
        tv = jnp.where(kcol == k, m, tv)
        ti = jnp.where(kcol == k, idx, ti)
        logits = jnp.where(pick, -jnp.inf, logits)
    e = jnp.exp(tv - tv[:, 0:1])
    ti_ref[...] = ti.astype(jnp.int32)
    tp_ref[...] = e / jnp.sum(e, axis=-1, keepdims=True)
    chosen = None
    for pick in picks:
        one = jnp.where(pick, 1.0, 0.0)
        chosen = one if chosen is None else chosen + one
    r = lax.broadcasted_iota(jnp.int32, (tm, tm), 0)
    c = lax.broadcasted_iota(jnp.int32, (tm, tm), 1)
    earlier = jnp.where(c < r, 1.0, 0.0).astype(BF16)
    base = cnt_ref[...] + _dot(earlier, chosen.astype(BF16))
    rk = jnp.zeros((tm, TOP_K), F32)
    for k, pick in enumerate(picks):
        rk = jnp.where(kcol == k, jnp.sum(jnp.where(pick, base, 0.0), axis=-1, keepdims=True), rk)
    rk_ref[...] = rk.astype(jnp.int32)
    cnt_ref[...] = cnt_ref[...] + jnp.sum(chosen, axis=0, keepdims=True)


def _merge(x2d, mod, branches, w_gate, w_branch, w_out, ln_g, ln_b, w_router, b_router, counts, tm, rows_per_seq, alpha):
    n, d = x2d.shape
    full = lambda a: pl.BlockSpec(a.shape, lambda i: (0,) * a.ndim)
    row = lambda w: pl.BlockSpec((tm, w), lambda i: (i, 0))
    return pl.pallas_call(
        functools.partial(_merge_kernel, alpha=alpha),
        grid=(n // tm,),
        in_specs=[row(d), _mod_spec(mod, tm, rows_per_seq)] + [row(b.shape[1]) for b in branches]
        + [full(w_gate), full(w_branch), full(w_out), full(ln_g), full(ln_b), full(w_router), full(b_router),
           full(counts)],
        out_specs=[row(d), row(d), row(TOP_K), row(TOP_K), row(TOP_K), full(counts)],
        out_shape=[jax.ShapeDtypeStruct((n, d), F32), jax.ShapeDtypeStruct((n, d), BF16),
                   jax.ShapeDtypeStruct((n, TOP_K), jnp.int32), jax.ShapeDtypeStruct((n, TOP_K), F32),
                   jax.ShapeDtypeStruct((n, TOP_K), jnp.int32), jax.ShapeDtypeStruct(counts.shape, F32)],
        compiler_params=_cparams(1),
    )(x2d, mod, *branches, w_gate, w_branch, w_out, ln_g, ln_b, w_router, b_router, counts)


def _expert_kernel(be_ref, nb_ref, x_ref, w1_ref, b1_ref, w2_ref, b2_ref, o_ref):
    @pl.when(pl.program_id(0) < nb_ref[0])
    def _():
        a = _dot(x_ref[...], w1_ref[...]) + b1_ref[...]
        glu = jnp.minimum(a, SWIGLU_LIMIT)
        lin = jnp.clip(a, -SWIGLU_LIMIT, SWIGLU_LIMIT) + 1.0
        act = glu * jax.nn.sigmoid(SWIGLU_ALPHA * glu) * pltpu.roll(lin, a.shape[1] - 1, 1)
        o_ref[...] = _dot(act.astype(BF16), w2_ref[...]) + b2_ref[...]


def _experts(block_exp, n_used, xs, w1, b1, w2x, b2):
    n_slots, d = xs.shape
    f2 = w1.shape[2]
    nb = n_slots // MOE_ROWS
    blk = lambda i, be, nu: (jnp.minimum(i, nu[0] - 1), 0)
    wspec = lambda s: pl.BlockSpec((None,) + s, lambda i, be, nu: (be[jnp.minimum(i, nu[0] - 1)], 0, 0))
    grid_spec = pltpu.PrefetchScalarGridSpec(
        num_scalar_prefetch=2,
        grid=(nb,),
        in_specs=[pl.BlockSpec((MOE_ROWS, d), blk),
                  wspec((d, f2)), wspec((1, f2)), wspec((f2, d)), wspec((1, d))],
        out_specs=pl.BlockSpec((MOE_ROWS, d), blk))
    return pl.pallas_call(
        _expert_kernel,
        grid_spec=grid_spec,
        out_shape=jax.ShapeDtypeStruct((n_slots, d), F32),
        compiler_params=_cparams(1),
    )(block_exp, n_used, xs, w1, b1, w2x, b2)


def _final_kernel(x_ref, mod_ref, y_ref, p_ref, g_ref, b_ref, o_ref, *, alpha):
    p = p_ref[...]
    f = None
    for k in range(TOP_K):
        term = p[:, k:k + 1] * y_ref[k]
        f = term if f is None else f + term
    o_ref[...] = _layer_norm(alpha * x_ref[...] + (1.0 + mod_ref[5]) * f, g_ref[...], b_ref[...])


def _final(x1, mod, y4, probs, ln_g, ln_b, tm, rows_per_seq, alpha):
    n, d = x1.shape
    full = lambda a: pl.BlockSpec(a.shape, lambda i: (0,) * a.ndim)
    return pl.pallas_call(
        functools.partial(_final_kernel, alpha=alpha),
        grid=(n // tm,),
        in_specs=[pl.BlockSpec((tm, d), lambda i: (i, 0)), _mod_spec(mod, tm, rows_per_seq),
                  pl.BlockSpec((TOP_K, tm, d), lambda i: (0, i, 0)),
                  pl.BlockSpec((tm, TOP_K), lambda i: (i, 0)), full(ln_g), full(ln_b)],
        out_specs=pl.BlockSpec((tm, d), lambda i: (i, 0)),
        out_shape=jax.ShapeDtypeStruct((n, d), F32),
        compiler_params=_cparams(1),
    )(x1, mod, y4, probs, ln_g, ln_b)


def _rope_tables(pos, rope):
    half = rope // 2
    inv = ROPE_BASE ** (-jnp.arange(half, dtype=F32) / half)
    ang = pos.astype(F32)[:, None] * inv[None, :]
    cos, sin = jnp.cos(ang), jnp.sin(ang)
    reps = LANES // rope
    return (jnp.tile(jnp.concatenate([cos, cos], -1), (1, reps)),
            jnp.tile(jnp.concatenate([-sin, sin], -1), (1, reps)))


def _swap_halves(w):
    half = w.shape[-1] // 2
    return jnp.concatenate([w[..., half:], w[..., :half]], axis=-1)


def _layer_weights(l, p, dims):
    d, dp, hd, q_lora, kv_lora, rope = dims
    w_in = p['w_in'][l]
    widths = (dp, 3 * hd, N_HEADS, q_lora, kv_lora, rope, 3 * hd, N_BRANCH * d)
    offs = [0]
    for w in widths:
        offs.append(offs[-1] + w)
    cols = [w_in[:, offs[i]:offs[i + 1]] for i in range(len(widths))]
    w_pool, w_fox, w_f, w_cq, w_ckv, w_kr, w_sb, w_gate = cols
    small = jnp.concatenate([w_kr, _swap_halves(w_kr), w_f, jnp.zeros((d, LANES - 2 * rope - N_HEADS), F32)], axis=1)
    w_a = jnp.concatenate([w_pool, w_fox, w_cq, w_ckv, w_sb, small], axis=1).astype(BF16)
    eye = jnp.eye(N_HEADS, dtype=F32)
    w_uq = p['w_mla_uq'][l]
    nope = w_uq.shape[-1] - rope
    pad = lambda w: jnp.pad(w, ((0, 0), (0, 0), (0, LANES - rope))).reshape(q_lora, N_HEADS * LANES)
    w_rope = w_uq[:, :, nope:]
    lw = {
        'w_a': w_a,
        'w_gate': w_gate.astype(BF16),
        'w_pool_bd': jnp.einsum('gcd,gh->gchd', p['w_pool_mix'][l], jnp.eye(len(POOL_WINDOWS), dtype=F32)
                                ).reshape(dp, dp).astype(BF16),
        'pool_scale': p['pool_scale'][l].reshape(1, dp),
        'fox_bias': jnp.zeros((1, LANES), F32).at[0, 2 * rope:2 * rope + N_HEADS].set(p['b_fox_forget'][l]),
        'g_q': p['mla_q_norm'][l].reshape(1, q_lora),
        'g_kv': p['mla_kv_norm'][l].reshape(1, kv_lora),
        'w_nope': w_uq[:, :, :nope].reshape(q_lora, N_HEADS * nope).astype(BF16),
        'w_rope': pad(w_rope).astype(BF16),
        'w_rope_sw': pad(_swap_halves(w_rope)).astype(BF16),
        'w_uk_bd': jnp.einsum('chn,hg->hngc', p['w_mla_uk'][l], eye).reshape(N_HEADS * nope, N_HEADS * kv_lora).astype(BF16),
        'w_uv_bd': jnp.einsum('chv,hg->hcgv', p['w_mla_uv'][l], eye).reshape(N_HEADS * kv_lora, -1).astype(BF16),
        'w_uv_all': p['w_mla_uv'][l].reshape(kv_lora, -1).astype(BF16),
        'w_branch': p['w_branch'][l].astype(BF16),
        'w_out': p['w_out'][l].astype(BF16),
        'ln1_g': p['ln1_g'][l].reshape(1, d), 'ln1_b': p['ln1_b'][l].reshape(1, d),
        'ln2_g': p['ln2_g'][l].reshape(1, d), 'ln2_b': p['ln2_b'][l].reshape(1, d),
        'b_router': p['b_router'][l].reshape(1, -1),
        'w1': p['w_exp1'][l].astype(BF16), 'b1': p['b_exp1'][l][:, None, :],
        'w2x': jnp.stack([p['w_exp2'][l], jnp.zeros_like(p['w_exp2'][l])], axis=2
                         ).reshape(p['w_exp2'].shape[1], -1, d).astype(BF16),
        'b2': p['b_exp2'][l][:, None, :],
        'w_router': p['w_router'][l].astype(BF16),
    }
    lw['widths'] = (dp, 3 * hd, q_lora + kv_lora, 3 * hd, LANES)
    lw['nope'] = nope
    return lw


def _routing(top_i, rank, counts):
    n = top_i.shape[0]
    nk = n * TOP_K
    n_experts = counts.shape[-1]
    sizes = counts.reshape(n_experts).astype(jnp.int32)
    padded = (sizes + MOE_ROWS - 1) // MOE_ROWS * MOE_ROWS
    ends = jnp.cumsum(padded)
    starts = ends - padded
    first = jnp.cumsum(sizes) - sizes
    dest = starts[top_i] + rank
    n_blocks = -(-(nk + n_experts * (MOE_ROWS - 1)) // MOE_ROWS)
    blk_start = jnp.arange(n_blocks, dtype=jnp.int32) * MOE_ROWS
    block_exp = jnp.minimum(jnp.sum(ends[None, :] <= blk_start[:, None], axis=1), n_experts - 1).astype(jnp.int32)
    order = jnp.argsort(top_i.reshape(nk))
    slot = jnp.arange(n_blocks * MOE_ROWS, dtype=jnp.int32)
    e_slot = jnp.repeat(block_exp, MOE_ROWS)
    r = slot - starts[e_slot]
    pair = order[jnp.clip(first[e_slot] + r, 0, nk - 1)]
    slot_tok = jnp.where(r < sizes[e_slot], pair // TOP_K, 0).astype(jnp.int32)
    n_used = (ends[-1] // MOE_ROWS).astype(jnp.int32).reshape(1)
    return slot_tok, dest, block_exp, n_used


def _head_rows(q, scale):
    db, dt, hd = q.shape
    dh = hd // N_HEADS
    q4 = jnp.transpose((q * scale).reshape(db, dt, N_HEADS, dh), (0, 2, 1, 3))
    q4 = jnp.pad(q4, ((0, 0), (0, 0), (0, DEC_ROWS - dt), (0, 0)))
    return q4.reshape(db, N_HEADS * DEC_ROWS, dh).astype(BF16)


def _head_out(o, dt):
    db, _, dv = o.shape
    o4 = jnp.transpose(o.reshape(db, N_HEADS, DEC_ROWS, dv)[:, :, :dt], (0, 2, 1, 3))
    return o4.reshape(db * dt, N_HEADS * dv).astype(BF16)


def _diag_heads(o, dt):
    db, _, hv = o.shape
    dv = hv // N_HEADS
    o5 = o.reshape(db, N_HEADS, dt, N_HEADS, dv)
    d = jnp.stack([o5[:, h, :, h, :] for h in range(N_HEADS)], axis=2)
    return d.reshape(db * dt, hv).astype(BF16)


def _new_page(kv, page):
    db, dt, w = kv.shape
    dh = w // (2 * N_HEADS)
    kv = jnp.transpose(kv.reshape(db, dt, 2, N_HEADS, dh), (0, 2, 3, 4, 1))
    return jnp.pad(kv, ((0, 0), (0, 0), (0, 0), (0, 0), (0, page - dt)))


def kernel(x_prompt, x_sample, c_prompt, c_sample, cache_fox_kv, cache_fox_logf, cache_mla, cache_sb_kv, state_pool, page_table, w_ada, b_ada, w_in, b_fox_forget, w_pool_mix, pool_scale, mla_q_norm, w_mla_uq, mla_kv_norm, w_mla_uk, w_mla_uv, w_branch, w_out, ln1_g, ln1_b, w_router, b_router, w_exp1, b_exp1, w_exp2, b_exp2, ln2_g, ln2_b):
    p = dict(w_in=w_in, b_fox_forget=b_fox_forget, w_pool_mix=w_pool_mix, pool_scale=pool_scale,
             mla_q_norm=mla_q_norm, w_mla_uq=w_mla_uq, mla_kv_norm=mla_kv_norm, w_mla_uk=w_mla_uk,
             w_mla_uv=w_mla_uv, w_branch=w_branch, w_out=w_out, ln1_g=ln1_g, ln1_b=ln1_b, w_router=w_router,
             b_router=b_router, w_exp1=w_exp1, b_exp1=b_exp1, w_exp2=w_exp2, b_exp2=b_exp2, ln2_g=ln2_g, ln2_b=ln2_b)
    b, t, d = x_prompt.shape
    db, dt, _ = x_sample.shape
    depth = w_ada.shape[0]
    n_pages = page_table.shape[1]
    page = cache_fox_kv.shape[2]
    past = n_pages * page
    dp = state_pool.shape[-1]
    hd = cache_fox_kv.shape[-1] * cache_fox_kv.shape[-2]
    dh = hd // N_HEADS
    q_lora = mla_q_norm.shape[1]
    kv_lora = mla_kv_norm.shape[1]
    rope = cache_mla.shape[-1] - kv_lora
    n_experts = w_router.shape[-1]
    alpha = (2 * depth) ** 0.25
    dims = (d, dp, hd, q_lora, kv_lora, rope)
    np_rows, ns_rows = b * t, db * dt
    tm = min(ROW_TILE, t)
    tile = min(ATT_TILE, t)
    sb_tile = min(SB_TILE, t)
    nq = t // tile

    mods = _ada(jnp.concatenate([c_prompt, c_sample], axis=0), w_ada, b_ada)
    cos_p, sin_p = _rope_tables(jnp.arange(t), rope)
    cos_s, sin_s = (jnp.tile(a, (db, 1)) for a in _rope_tables(past + jnp.arange(dt), rope))
    fox_kv_t = jnp.transpose(cache_fox_kv, (0, 1, 3, 4, 5, 2))
    sb_kv_t = jnp.transpose(cache_sb_kv, (0, 1, 3, 4, 5, 2))
    mla_t = jnp.transpose(cache_mla, (0, 1, 3, 2))
    logf_t = jnp.transpose(cache_fox_logf, (0, 1, 3, 2))
    sfx = _fox_sfx(logf_t.reshape(-1, page)).reshape(logf_t.shape[:3] + (2 * page,))
    zero_counts = jnp.zeros((1, n_experts), F32)

    xp = x_prompt.reshape(np_rows, d)
    xs = x_sample.reshape(ns_rows, d)
    outs = {k: [] for k in ('fox_p', 'fox_s', 'lf_p', 'lf_s', 'mla_p', 'mla_s', 'sb_p', 'sb_s', 'pool_p', 'pool_s')}
    for l in range(depth):
        lw = _layer_weights(l, p, dims)
        mod_p = jnp.transpose(mods[l, :, :b], (1, 0, 2))[:, :, None, :]
        mod_s = jnp.repeat(mods[l, :, b:], dt, axis=1)[None]
        mla_scale = (lw['nope'] + rope) ** -0.5

        u_pool, u_fox, u_mla, u_sb, u_small = _inproj(xp, mod_p, lw['w_a'], lw['widths'], tm, t)
        br_a, pool_new = _pool(u_pool.reshape(b, t, dp), jnp.zeros((b, POOL_BUF, dp), F32),
                               lw['w_pool_bd'], lw['pool_scale'], 0)
        lf, cum = _fox_prep(u_small.reshape(b, t, LANES), lw['fox_bias'], 2 * rope)
        c_s = jnp.transpose(cum, (0, 2, 1)).reshape(b, N_HEADS, nq, 1, tile)
        br_b = _fox_attn(u_fox.reshape(b, t, 3 * hd), cum, c_s, tile)
        q_cat, lat_pad, lat_new = _mla_prep(u_mla, u_small, cos_p, sin_p, lw['g_q'], lw['g_kv'], lw['w_nope'],
                                            lw['w_rope'], lw['w_rope_sw'], lw['w_uk_bd'], tm, rope)
        br_c = _mla_attn(q_cat.reshape(b, t, -1), lat_pad.reshape(b, t, -1), lw['w_uv_bd'], tile, kv_lora, mla_scale)
        br_d = _sb_attn(u_sb.reshape(b, t, 3 * hd), sb_tile)
        branches = [br_a.reshape(np_rows, dp), br_b.reshape(np_rows, hd), br_c.reshape(np_rows, hd),
                    br_d.reshape(np_rows, hd)]
        x1_p, h2_p, ti_p, tp_p, rk_p, counts = _merge(
            xp, mod_p, branches, lw['w_gate'], lw['w_branch'], lw['w_out'], lw['ln1_g'], lw['ln1_b'],
            lw['w_router'], lw['b_router'], zero_counts, tm, t, alpha)
        outs['fox_p'].append(u_fox[:, hd:].reshape(b, t, 2, N_HEADS, dh))
        outs['lf_p'].append(lf)
        outs['mla_p'].append(lat_new.reshape(b, t, kv_lora + rope))
        outs['sb_p'].append(u_sb[:, hd:].reshape(b, t, 2, N_HEADS, dh))
        outs['pool_p'].append(pool_new)

        u_pool, u_fox, u_mla, u_sb, u_small = _inproj(xs, mod_s, lw['w_a'], lw['widths'], ns_rows, dt)
        br_a, pool_new = _pool(u_pool.reshape(db, dt, dp), state_pool[l], lw['w_pool_bd'], lw['pool_scale'], past)
        lf, cum = _fox_prep(u_small.reshape(db, dt, LANES), lw['fox_bias'], 2 * rope)
        u_fox3 = u_fox.reshape(db, dt, 3 * hd)
        g_new = jnp.broadcast_to(-jnp.transpose(cum, (0, 2, 1))[:, :, None, :], (db, N_HEADS, DEC_ROWS, dt))
        g_new = jnp.pad(g_new.reshape(db, N_HEADS * DEC_ROWS, dt), ((0, 0), (0, 0), (0, LANES - dt)))
        o_fox = _fox_dec(l, _head_rows(u_fox3[:, :, :hd], dh ** -0.5), _new_page(u_fox3[:, :, hd:], page), g_new,
                         fox_kv_t, sfx, page_table, dt)
        q_cat, lat_pad, lat_new = _mla_prep(u_mla, u_small, cos_s, sin_s, lw['g_q'], lw['g_kv'], lw['w_nope'],
                                            lw['w_rope'], lw['w_rope_sw'], lw['w_uk_bd'], ns_rows, rope)
        wq = kv_lora + LANES
        q_rows = q_cat.reshape(db, dt, N_HEADS, wq)[..., :kv_lora + rope]
        q_rows = jnp.transpose(q_rows, (0, 2, 1, 3)).reshape(db, N_HEADS * dt, kv_lora + rope)
        lat_t = jnp.pad(jnp.transpose(lat_new.reshape(db, dt, -1), (0, 2, 1)), ((0, 0), (0, 0), (0, page - dt)))
        o_mla = _mla_dec(l, q_rows, lat_t, lw['w_uv_all'], mla_t, page_table, dt, kv_lora, mla_scale)
        u_sb3 = u_sb.reshape(db, dt, 3 * hd)
        o_sb = _sb_dec(l, _head_rows(u_sb3[:, :, :hd], dh ** -0.5), _new_page(u_sb3[:, :, hd:], page),
                       sb_kv_t, page_table, dt)
        branches = [br_a.reshape(ns_rows, dp), _head_out(o_fox, dt), _diag_heads(o_mla, dt), _head_out(o_sb, dt)]
        x1_s, h2_s, ti_s, tp_s, rk_s, counts = _merge(
            xs, mod_s, branches, lw['w_gate'], lw['w_branch'], lw['w_out'], lw['ln1_g'], lw['ln1_b'],
            lw['w_router'], lw['b_router'], counts, ns_rows, dt, alpha)
        outs['fox_s'].append(u_fox3[:, :, hd:].reshape(db, dt, 2, N_HEADS, dh))
        outs['lf_s'].append(lf)
        outs['mla_s'].append(lat_new.reshape(db, dt, kv_lora + rope))
        outs['sb_s'].append(u_sb3[:, :, hd:].reshape(db, dt, 2, N_HEADS, dh))
        outs['pool_s'].append(pool_new)

        h2 = jnp.concatenate([h2_p, h2_s], axis=0)
        slot_tok, dest, block_exp, n_used = _routing(jnp.concatenate([ti_p, ti_s], axis=0),
                                                     jnp.concatenate([rk_p, rk_s], axis=0), counts)
        y_slots = _experts(block_exp, n_used, h2[slot_tok], lw['w1'], lw['b1'], lw['w2x'], lw['b2'])
        y_p = y_slots[dest[:np_rows].T.reshape(-1)].reshape(TOP_K, np_rows, d)
        y_s = y_slots[dest[np_rows:].T.reshape(-1)].reshape(TOP_K, ns_rows, d)
        xp = _final(x1_p, mod_p, y_p, tp_p, lw['ln2_g'], lw['ln2_b'], tm, t, alpha)
        xs = _final(x1_s, mod_s, y_s, tp_s, lw['ln2_g'], lw['ln2_b'], ns_rows, dt, alpha)

    st = lambda k: jnp.stack(outs[k])
    return (xp.reshape(b, t, d), xs.reshape(db, dt, d), st('fox_p'), st('fox_s'), st('lf_p'), st('lf_s'),
            st('mla_p'), st('mla_s'), st('sb_p'), st('sb_s'), st('pool_p'), st('pool_s'))
```

```python
import functools

import jax
import jax.numpy as jnp
from jax import lax
from jax.experimental import pallas as pl
from jax.experimental.pallas import tpu as pltpu

F32 = jnp.float32
BF16 = jnp.bfloat16

POOL_WINDOWS = (2, 4, 8, 16)
POOL_BUF = max(POOL_WINDOWS) - 1
N_HEADS = 4
N_BRANCH = 4
N_MOD = 6
TOP_K = 4
ROPE_BASE = 10000.0
SWIGLU_ALPHA = 1.702
SWIGLU_LIMIT = 7.0
LN_EPS = 1e-5
RMS_EPS = 1e-6

LANES = 128
SUBLANES = 8
VMEM_LIMIT = 56 * 1024 * 1024

NEG = -1e30

ROW_TILE = 512
ATT_TILE = 512
SB_TILE = 256
PAGES_PER_STEP = 16
MOE_ROWS = 256
DEC_ROWS = 16


def _cparams(n_axes):
    return pltpu.CompilerParams(dimension_semantics=("arbitrary",) * n_axes, vmem_limit_bytes=VMEM_LIMIT)


def _nt(a, b):
    return lax.dot_general(a, b, (((1,), (1,)), ((), ())), preferred_element_type=F32)


def _dot(a, b):
    return jnp.dot(a, b, preferred_element_type=F32)


def _split3(x):
    hi = x.astype(BF16)
    r = x - hi.astype(F32)
    mid = r.astype(BF16)
    lo = (r - mid.astype(F32)).astype(BF16)
    return hi, mid, lo


def _dot_f32_lhs(x, m01):
    hi, mid, lo = _split3(x)
    return _dot(hi, m01) + _dot(mid, m01) + _dot(lo, m01)


def _dot_f32_lhs2(x, m01):
    hi = x.astype(BF16)
    lo = (x - hi.astype(F32)).astype(BF16)
    return _dot(hi, m01) + _dot(lo, m01)


def _dot_f32_rhs(m01, x):
    hi, mid, lo = _split3(x)
    return _dot(m01, hi) + _dot(m01, mid) + _dot(m01, lo)


def _log_sigmoid(x):
    return jnp.minimum(x, 0.0) - jnp.log1p(jnp.exp(-jnp.abs(x)))


def _mod_spec(mod, tm, rows_per_seq):
    d = mod.shape[-1]
    if mod.shape[2] == 1:
        return pl.BlockSpec((None, N_MOD, 1, d), lambda i: ((i * tm) // rows_per_seq, 0, 0, 0))
    return pl.BlockSpec((None, N_MOD, tm, d), lambda i: (0, 0, i, 0))


def _ada_kernel(c_ref, w_ref, b_ref, o_ref):
    c = c_ref[...]
    s = c * jax.nn.sigmoid(c)
    o_ref[...] = _dot(s.astype(BF16), w_ref[...].astype(BF16)) + b_ref[...]


def _ada(c_all, w_ada, b_ada):
    depth, d, _ = w_ada.shape
    r = c_all.shape[0]
    b4 = b_ada.reshape(depth, N_MOD, 1, d)
    return pl.pallas_call(
        _ada_kernel,
        grid=(depth, N_MOD),
        in_specs=[pl.BlockSpec((r, d), lambda l, k: (0, 0)),
                  pl.BlockSpec((None, d, d), lambda l, k: (l, 0, k)),
                  pl.BlockSpec((None, None, 1, d), lambda l, k: (l, k, 0, 0))],
        out_specs=pl.BlockSpec((None, None, r, d), lambda l, k: (l, k, 0, 0)),
        out_shape=jax.ShapeDtypeStruct((depth, N_MOD, r, d), F32),
        compiler_params=_cparams(2),
    )(c_all, w_ada, b4)


def _inproj_kernel(x_ref, mod_ref, w_ref, *out_refs, widths):
    h = x_ref[...] * (1.0 + mod_ref[1]) + mod_ref[0]
    u = _dot(h.astype(BF16), w_ref[...])
    off = 0
    for o_ref, w in zip(out_refs, widths):
        o_ref[...] = u[:, off:off + w]
        off += w


def _inproj(x2d, mod, w_a, widths, tm, rows_per_seq):
    n, d = x2d.shape
    return pl.pallas_call(
        functools.partial(_inproj_kernel, widths=widths),
        grid=(n // tm,),
        in_specs=[pl.BlockSpec((tm, d), lambda i: (i, 0)),
                  _mod_spec(mod, tm, rows_per_seq),
                  pl.BlockSpec(w_a.shape, lambda i: (0, 0))],
        out_specs=[pl.BlockSpec((tm, w), lambda i: (i, 0)) for w in widths],
        out_shape=[jax.ShapeDtypeStruct((n, w), F32) for w in widths],
        compiler_params=_cparams(1),
    )(x2d, mod, w_a)


def _pool_kernel(u_ref, pre_ref, w_ref, sc_ref, o_ref, new_ref, z_ref, *, t, pos0):
    total, dp = z_ref.shape
    base = POOL_BUF + 1
    z_ref[0:1, :] = jnp.zeros((1, dp), F32)
    z_ref[1:base, :] = pre_ref[...]
    z_ref[base:base + t, :] = u_ref[...]
    if total > base + t:
        z_ref[base + t:total, :] = jnp.zeros((total - base - t, dp), F32)
    z = z_ref[...]
    sums = []
    s = z
    for step in (1, 2, 4, 8):
        s = s + pltpu.roll(s, step, 0)
        sums.append(s)
    lane = lax.broadcasted_iota(jnp.int32, (t, dp), 1)
    group = dp // len(POOL_WINDOWS)
    win = sums[-1][base:base + t]
    wlen = jnp.full((t, dp), float(POOL_WINDOWS[-1]), F32)
    for g in range(len(POOL_WINDOWS) - 2, -1, -1):
        sel = lane < (g + 1) * group
        win = jnp.where(sel, sums[g][base:base + t], win)
        wlen = jnp.where(sel, float(POOL_WINDOWS[g]), wlen)
    pos = (pos0 + lax.broadcasted_iota(jnp.int32, (t, dp), 0)).astype(F32)
    cnt = jnp.minimum(wlen, pos + 1.0)
    pooled = win / cnt - u_ref[...]
    o_ref[...] = (_dot(pooled.astype(BF16), w_ref[...]) * sc_ref[...]).astype(o_ref.dtype)
    new_ref[...] = z_ref[t + 1:t + 1 + POOL_BUF, :]


def _pool(u_pool, prefix, w_bd, scale, pos0):
    b, t, dp = u_pool.shape
    total = -(-(POOL_BUF + 1 + t) // SUBLANES) * SUBLANES
    return pl.pallas_call(
        functools.partial(_pool_kernel, t=t, pos0=pos0),
        grid=(b,),
        in_specs=[pl.BlockSpec((None, t, dp), lambda i: (i, 0, 0)),
                  pl.BlockSpec((None, POOL_BUF, dp), lambda i: (i, 0, 0)),
                  pl.BlockSpec((dp, dp), lambda i: (0, 0)),
                  pl.BlockSpec((1, dp), lambda i: (0, 0))],
        out_specs=[pl.BlockSpec((None, t, dp), lambda i: (i, 0, 0)),
                   pl.BlockSpec((None, POOL_BUF, dp), lambda i: (i, 0, 0))],
        out_shape=[jax.ShapeDtypeStruct((b, t, dp), BF16),
                   jax.ShapeDtypeStruct((b, POOL_BUF, dp), F32)],
        scratch_shapes=[pltpu.VMEM((total, dp), F32)],
        compiler_params=_cparams(1),
    )(u_pool, prefix, w_bd, scale)


def _fox_prep_kernel(s_ref, b_ref, lf_ref, c_ref, *, t, col):
    x = s_ref[...] + b_ref[...]
    lf = _log_sigmoid(x)
    lf_ref[...] = lf[:, col:col + N_HEADS]
    if t % LANES == 0:
        r = lax.broadcasted_iota(jnp.int32, (LANES, LANES), 0)
        c = lax.broadcasted_iota(jnp.int32, (LANES, LANES), 1)
        tri = jnp.where(r >= c, 1.0, 0.0).astype(BF16)
        carry = jnp.zeros((1, LANES), F32)
        for ch in range(t // LANES):
            cs = _dot_f32_rhs(tri, lf[ch * LANES:(ch + 1) * LANES]) + carry
            c_ref[ch * LANES:(ch + 1) * LANES, :] = cs[:, col:col + N_HEADS]
            carry = cs[LANES - 1:LANES, :]
    else:
        acc = jnp.zeros((1, LANES), F32)
        for i in range(t):
            acc = acc + lf[i:i + 1]
            c_ref[i:i + 1, :] = acc[:, col:col + N_HEADS]


def _fox_prep(small, bias_row, col):
    b, t, w = small.shape
    return pl.pallas_call(
        functools.partial(_fox_prep_kernel, t=t, col=col),
        grid=(b,),
        in_specs=[pl.BlockSpec((None, t, w), lambda i: (i, 0, 0)),
                  pl.BlockSpec((1, w), lambda i: (0, 0))],
        out_specs=[pl.BlockSpec((None, t, N_HEADS), lambda i: (i, 0, 0))] * 2,
        out_shape=[jax.ShapeDtypeStruct((b, t, N_HEADS), F32)] * 2,
        compiler_params=_cparams(1),
    )(small, bias_row)


def _softmax_step(carry, s, v):
    m, l, acc = carry
    m_new = jnp.maximum(m, jnp.max(s, axis=-1, keepdims=True))
    p = jnp.exp(s - m_new)
    alpha = jnp.exp(m - m_new)
    l = alpha * l + jnp.sum(p, axis=-1, keepdims=True)
    acc = alpha * acc + _dot(p.astype(BF16), v)
    return m_new, l, acc


def _causal_mask(tq, tk, strict):
    r = lax.broadcasted_iota(jnp.int32, (tq, tk), 0)
    c = lax.broadcasted_iota(jnp.int32, (tq, tk), 1)
    return (c < r) if strict else (c <= r)


def _fox_attn_kernel(q_ref, k_ref, v_ref, ct_ref, cs_ref, o_ref, *, tile, dh):
    i = pl.program_id(1)
    scale = dh ** -0.5
    mask = _causal_mask(tile, tile, False)
    heads = [slice(h * dh, (h + 1) * dh) for h in range(N_HEADS)]
    qs = [(q_ref[:, sl] * scale).astype(BF16) for sl in heads]
    cts = [ct_ref[:, h:h + 1] for h in range(N_HEADS)]

    def step(j, carry, masked):
        rows = pl.ds(pl.multiple_of(j * tile, tile), tile)
        out = []
        for h, sl in enumerate(heads):
            s = _nt(qs[h], k_ref[rows, sl].astype(BF16)) + (cts[h] - cs_ref[h, j])
            if masked:
                s = jnp.where(mask, s, NEG)
            out.append(_softmax_step(carry[h], s, v_ref[rows, sl].astype(BF16)))
        return tuple(out)

    init = tuple((jnp.full((tile, 1), NEG, F32), jnp.zeros((tile, 1), F32), jnp.zeros((tile, dh), F32))
                 for _ in heads)
    carry = lax.fori_loop(0, i, lambda j, c: step(j, c, False), init)
    carry = step(i, carry, True)
    for sl, (m, l, acc) in zip(heads, carry):
        o_ref[:, sl] = (acc / l).astype(o_ref.dtype)


def _fox_attn(u_fox, c_t, c_s, tile):
    b, t, w = u_fox.shape
    hd = w // 3
    dh = hd // N_HEADS
    nq = t // tile
    return pl.pallas_call(
        functools.partial(_fox_attn_kernel, tile=tile, dh=dh),
        grid=(b, nq),
        in_specs=[pl.BlockSpec((None, tile, hd), lambda bi, i: (bi, i, 0)),
                  pl.BlockSpec((None, t, hd), lambda bi, i: (bi, 0, 1)),
                  pl.BlockSpec((None, t, hd), lambda bi, i: (bi, 0, 2)),
                  pl.BlockSpec((None, tile, N_HEADS), lambda bi, i: (bi, i, 0)),
                  pl.BlockSpec((None, N_HEADS, nq, 1, tile), lambda bi, i: (bi, 0, 0, 0, 0))],
        out_specs=pl.BlockSpec((None, tile, hd), lambda bi, i: (bi, i, 0)),
        out_shape=jax.ShapeDtypeStruct((b, t, hd), BF16),
        compiler_params=_cparams(2),
    )(u_fox, u_fox, u_fox, c_t, c_s)


def _mla_attn_kernel(q_ref, lat_ref, wuv_ref, o_ref, *, tile, lora, scale):
    i = pl.program_id(1)
    wq = q_ref.shape[1] // N_HEADS
    q = jnp.concatenate([q_ref[:, h * wq:(h + 1) * wq] for h in range(N_HEADS)], axis=0)
    rows_all = N_HEADS * tile
    r = lax.broadcasted_iota(jnp.int32, (rows_all, tile), 0) % tile
    c = lax.broadcasted_iota(jnp.int32, (rows_all, tile), 1)
    mask = c <= r

    def step(j, carry, masked):
        lat = lat_ref[pl.ds(pl.multiple_of(j * tile, tile), tile), :]
        s = _nt(q, lat) * scale
        if masked:
            s = jnp.where(mask, s, NEG)
        return _softmax_step(carry, s, lat[:, :lora])

    init = (jnp.full((rows_all, 1), NEG, F32), jnp.zeros((rows_all, 1), F32), jnp.zeros((rows_all, lora), F32))
    carry = lax.fori_loop(0, i, lambda j, c: step(j, c, False), init)
    m, l, acc = step(i, carry, True)
    o = (acc / l).astype(BF16)
    o = jnp.concatenate([o[h * tile:(h + 1) * tile] for h in range(N_HEADS)], axis=-1)
    o_ref[...] = _dot(o, wuv_ref[...]).astype(o_ref.dtype)


def _mla_attn(q_cat, lat_pad, w_uv_bd, tile, lora, scale):
    b, t, wq = q_cat.shape
    wl = lat_pad.shape[-1]
    wo = w_uv_bd.shape[1]
    return pl.pallas_call(
        functools.partial(_mla_attn_kernel, tile=tile, lora=lora, scale=scale),
        grid=(b, t // tile),
        in_specs=[pl.BlockSpec((None, tile, wq), lambda bi, i: (bi, i, 0)),
                  pl.BlockSpec((None, t, wl), lambda bi, i: (bi, 0, 0)),
                  pl.BlockSpec(w_uv_bd.shape, lambda bi, i: (0, 0))],
        out_specs=pl.BlockSpec((None, tile, wo), lambda bi, i: (bi, i, 0)),
        out_shape=jax.ShapeDtypeStruct((b, t, wo), BF16),
        compiler_params=_cparams(2),
    )(q_cat, lat_pad, w_uv_bd)


def _suffix_matrix(n, with_total):
    j = lax.broadcasted_iota(jnp.int32, (n, n), 0)
    s = lax.broadcasted_iota(jnp.int32, (n, n), 1)
    m = jnp.where(j > s, 1.0, 0.0).astype(BF16)
    if with_total:
        m = jnp.concatenate([m, jnp.ones((n, n), BF16)], axis=1)
    return m


def _row_sum(x):
    return jnp.sum(x, axis=-1, keepdims=True)


def _sb_weights(z, carry_a, msuf, valid):
    ls = _log_sigmoid(z)
    lfail = ls - z
    if valid is not None:
        lfail = jnp.where(valid, lfail, 0.0)
    w = jnp.exp(ls + _dot_f32_lhs2(lfail, msuf) + carry_a)
    if valid is not None:
        w = jnp.where(valid, w, 0.0)
    return w, carry_a + _row_sum(lfail)


def _sb_block(z, v, carry_a, acc, msuf, valid):
    w, carry_a = _sb_weights(z, carry_a, msuf, valid)
    return carry_a, acc + _dot(w.astype(BF16), v)


def _sb_attn_kernel(q_ref, k_ref, v_ref, o_ref, *, tile, dh):
    i = pl.program_id(1)
    scale = dh ** -0.5
    valid = _causal_mask(tile, tile, True)
    msuf = _suffix_matrix(tile, False)
    heads = [slice(h * dh, (h + 1) * dh) for h in range(N_HEADS)]
    qs = [(q_ref[:, sl] * scale).astype(BF16) for sl in heads]

    def step(j, carry, mask):
        rows = pl.ds(pl.multiple_of(j * tile, tile), tile)
        return tuple(_sb_block(_nt(qs[h], k_ref[rows, sl].astype(BF16)), v_ref[rows, sl].astype(BF16),
                               carry[h][0], carry[h][1], msuf, mask) for h, sl in enumerate(heads))

    init = tuple((jnp.zeros((tile, 1), F32), jnp.zeros((tile, dh), F32)) for _ in heads)
    carry = step(i, init, valid)
    carry = lax.fori_loop(0, i, lambda t, c: step(i - 1 - t, c, None), carry)
    for sl, (a, acc) in zip(heads, carry):
        o_ref[:, sl] = acc.astype(o_ref.dtype)


def _sb_attn(u_sb, tile):
    b, t, w = u_sb.shape
    hd = w // 3
    dh = hd // N_HEADS
    return pl.pallas_call(
        functools.partial(_sb_attn_kernel, tile=tile, dh=dh),
        grid=(b, t // tile),
        in_specs=[pl.BlockSpec((None, tile, hd), lambda bi, i: (bi, i, 0)),
                  pl.BlockSpec((None, t, hd), lambda bi, i: (bi, 0, 1)),
                  pl.BlockSpec((None, t, hd), lambda bi, i: (bi, 0, 2))],
        out_specs=pl.BlockSpec((None, tile, hd), lambda bi, i: (bi, i, 0)),
        out_shape=jax.ShapeDtypeStruct((b, t, hd), BF16),
        compiler_params=_cparams(2),
    )(u_sb, u_sb, u_sb)


def _rms(x, g):
    return x * lax.rsqrt(jnp.mean(jnp.square(x), axis=-1, keepdims=True) + RMS_EPS) * g


def _mla_prep_kernel(u_ref, s_ref, cos_ref, sin_ref, gq_ref, gkv_ref, wn_ref, wr_ref, wrs_ref, wuk_ref,
                     qcat_ref, latpad_ref, lat_ref, *, q_lora, kv_lora, rope):
    cos = cos_ref[...]
    sin = sin_ref[...]
    cq = _rms(u_ref[:, :q_lora], gq_ref[...]).astype(BF16)
    q_nope = _dot(cq, wn_ref[...])
    q_rope = _dot(cq, wr_ref[...])
    q_rope_sw = _dot(cq, wrs_ref[...])
    q_abs = _dot(q_nope.astype(BF16), wuk_ref[...])
    pieces = []
    for h in range(N_HEADS):
        blk = slice(h * LANES, (h + 1) * LANES)
        pieces.append(q_abs[:, h * kv_lora:(h + 1) * kv_lora])
        pieces.append(q_rope[:, blk] * cos + q_rope_sw[:, blk] * sin)
    qcat_ref[...] = jnp.concatenate(pieces, axis=-1).astype(qcat_ref.dtype)
    ckv = _rms(u_ref[:, q_lora:q_lora + kv_lora], gkv_ref[...])
    small = s_ref[...]
    kr = small * cos + pltpu.roll(small, LANES - rope, 1) * sin
    lane = lax.broadcasted_iota(jnp.int32, kr.shape, 1)
    kr = jnp.where(lane < rope, kr, 0.0)
    latpad_ref[...] = jnp.concatenate([ckv, kr], axis=-1).astype(latpad_ref.dtype)
    lat_ref[...] = jnp.concatenate([ckv, kr[:, :rope]], axis=-1)


def _mla_prep(u_mla, small, cos, sin, g_q, g_kv, w_nope, w_rope, w_rope_sw, w_uk_bd, tm, rope):
    n, wm = u_mla.shape
    q_lora = g_q.shape[1]
    kv_lora = g_kv.shape[1]
    ncs = cos.shape[0] // tm
    wq = N_HEADS * (kv_lora + LANES)
    full = lambda a: pl.BlockSpec(a.shape, lambda i: (0, 0))
    return pl.pallas_call(
        functools.partial(_mla_prep_kernel, q_lora=q_lora, kv_lora=kv_lora, rope=rope),
        grid=(n // tm,),
        in_specs=[pl.BlockSpec((tm, wm), lambda i: (i, 0)),
                  pl.BlockSpec((tm, LANES), lambda i: (i, 0)),
                  pl.BlockSpec((tm, LANES), lambda i: (i % ncs, 0)),
                  pl.BlockSpec((tm, LANES), lambda i: (i % ncs, 0)),
                  full(g_q), full(g_kv), full(w_nope), full(w_rope), full(w_rope_sw), full(w_uk_bd)],
        out_specs=[pl.BlockSpec((tm, wq), lambda i: (i, 0)),
                   pl.BlockSpec((tm, kv_lora + LANES), lambda i: (i, 0)),
                   pl.BlockSpec((tm, kv_lora + rope), lambda i: (i, 0))],
        out_shape=[jax.ShapeDtypeStruct((n, wq), BF16),
                   jax.ShapeDtypeStruct((n, kv_lora + LANES), BF16),
                   jax.ShapeDtypeStruct((n, kv_lora + rope), F32)],
        compiler_params=_cparams(1),
    )(u_mla, small, cos, sin, g_q, g_kv, w_nope, w_rope, w_rope_sw, w_uk_bd)


def _page_specs(block, layer, n_pages, pps):
    zeros = (0,) * len(block)

    def spec(r):
        return pl.BlockSpec((None, None) + block,
                            lambda b, c, pt, r=r: (layer, pt[b * n_pages + (n_pages - 1 - (c * pps + r))]) + zeros)
    return [spec(r) for r in range(pps)]


def _new_token_mask(rows, n_new, strict):
    r = lax.broadcasted_iota(jnp.int32, (rows, LANES), 0) % (rows // N_HEADS)
    c = lax.broadcasted_iota(jnp.int32, (rows, LANES), 1)
    limit = jnp.minimum(r if strict else r + 1, n_new)
    return c < limit


def _head_scores(q, kv_refs):
    qr = q.shape[0] // N_HEADS
    ss, vs = [], []
    for h in range(N_HEADS):
        kt = jnp.concatenate([ref[0, h].astype(BF16) for ref in kv_refs], axis=1)
        ss.append(_dot(q[h * qr:(h + 1) * qr], kt))
        vs.append(jnp.concatenate([ref[1, h].astype(BF16) for ref in kv_refs], axis=1))
    return jnp.concatenate(ss, axis=0), vs


def _head_pv(p, vs):
    qr = p.shape[0] // N_HEADS
    return jnp.concatenate([_nt(p[h * qr:(h + 1) * qr].astype(BF16), vs[h]) for h in range(N_HEADS)], axis=0)


def _stage_new_rows(pad_ref, new_ref):
    pad_ref[...] = jnp.zeros(pad_ref.shape, F32)
    pad_ref[0:new_ref.shape[0], :] = new_ref[...]


def _head_scores_rows(q, pad_ref):
    qr = q.shape[0] // N_HEADS
    hd = pad_ref.shape[1] // 2
    dh = hd // N_HEADS
    ss = [_nt(q[h * qr:(h + 1) * qr], pad_ref[:, h * dh:(h + 1) * dh].astype(BF16)) for h in range(N_HEADS)]
    vs = [pad_ref[:, hd + h * dh:hd + (h + 1) * dh].astype(BF16) for h in range(N_HEADS)]
    return jnp.concatenate(ss, axis=0), vs


def _head_pv_rows(p, vs):
    qr = p.shape[0] // N_HEADS
    return jnp.concatenate([_dot(p[h * qr:(h + 1) * qr].astype(BF16), vs[h]) for h in range(N_HEADS)], axis=0)


def _fox_sfx_kernel(x_ref, m_ref, o_ref):
    o_ref[...] = _dot_f32_lhs(x_ref[...], m_ref[...])


def _fox_sfx(logf_rows):
    n, page = logf_rows.shape
    tr = 2048 if n % 2048 == 0 else n
    m = _suffix_matrix(page, True)
    return pl.pallas_call(
        _fox_sfx_kernel,
        grid=(n // tr,),
        in_specs=[pl.BlockSpec((tr, page), lambda i: (i, 0)), pl.BlockSpec(m.shape, lambda i: (0, 0))],
        out_specs=pl.BlockSpec((tr, 2 * page), lambda i: (i, 0)),
        out_shape=jax.ShapeDtypeStruct((n, 2 * page), F32),
        compiler_params=_cparams(1),
    )(logf_rows, m)


def _softmax_update(m_ref, l_ref, acc_ref, s, pv, first):
    smax = jnp.max(s, axis=-1, keepdims=True)
    if first:
        m_new = smax
    else:
        m_old = m_ref[...]
        m_new = jnp.maximum(m_old, smax)
        alpha = jnp.exp(m_old - m_new)
    p = jnp.exp(s - m_new)
    l = jnp.sum(p, axis=-1, keepdims=True)
    acc = pv(p)
    if not first:
        l = l + alpha * l_ref[...]
        acc = acc + alpha * acc_ref[...]
    m_ref[...] = m_new
    l_ref[...] = l
    acc_ref[...] = acc
    return l, acc


def _fox_dec_kernel(pt_ref, q_ref, kvn_ref, gn_ref, *rest, pps, n_new):
    kv_refs, sfx_refs = rest[:pps], rest[pps:2 * pps]
    o_ref, m_ref, l_ref, acc_ref, car_ref, pad_ref = rest[2 * pps:]
    c = pl.program_id(1)
    q = q_ref[...]
    rows = q.shape[0]
    qr = rows // N_HEADS

    @pl.when(c == 0)
    def _():
        _stage_new_rows(pad_ref, kvn_ref)
        s, vs = _head_scores_rows(q, pad_ref)
        s = jnp.where(_new_token_mask(rows, n_new, False), s + gn_ref[...], NEG)
        _softmax_update(m_ref, l_ref, acc_ref, s, lambda p: _head_pv_rows(p, vs), True)
        car_ref[...] = jnp.zeros(car_ref.shape, F32)

    car = car_ref[0:N_HEADS, :]
    biases = []
    for r in range(pps):
        blk = sfx_refs[r][...]
        biases.append(blk[:, :LANES] + car)
        car = car + blk[:, LANES:]
    car_ref[0:N_HEADS, :] = car
    bias = jnp.concatenate(biases, axis=1)
    s, vs = _head_scores(q, kv_refs)
    s = s + jnp.concatenate([jnp.broadcast_to(bias[h:h + 1, :], (qr, bias.shape[1])) for h in range(N_HEADS)], axis=0)
    l, acc = _softmax_update(m_ref, l_ref, acc_ref, s, lambda p: _head_pv(p, vs), False)

    @pl.when(c == pl.num_programs(1) - 1)
    def _():
        o_ref[...] = acc / l


def _fox_dec(layer, q_rows, kv_new, g_new, cache_kv, sfx, page_table, n_new):
    db, rows, dh = q_rows.shape
    n_pages = page_table.shape[1]
    pps = min(PAGES_PER_STEP, n_pages)
    kv_block = cache_kv.shape[2:]
    sfx_block = sfx.shape[2:]
    grid_spec = pltpu.PrefetchScalarGridSpec(
        num_scalar_prefetch=1,
        grid=(db, n_pages // pps),
        in_specs=[pl.BlockSpec((None, rows, dh), lambda b, c, pt: (b, 0, 0)),
                  pl.BlockSpec((None,) + kv_new.shape[1:], lambda b, c, pt: (b, 0, 0)),
                  pl.BlockSpec((None, rows, LANES), lambda b, c, pt: (b, 0, 0))]
        + _page_specs(kv_block, layer, n_pages, pps) + _page_specs(sfx_block, layer, n_pages, pps),
        out_specs=pl.BlockSpec((None, rows, dh), lambda b, c, pt: (b, 0, 0)),
        scratch_shapes=[pltpu.VMEM((rows, 1), F32), pltpu.VMEM((rows, 1), F32), pltpu.VMEM((rows, dh), F32),
                        pltpu.VMEM((SUBLANES, LANES), F32), pltpu.VMEM((kv_block[-1], kv_new.shape[2]), F32)])
    return pl.pallas_call(
        functools.partial(_fox_dec_kernel, pps=pps, n_new=n_new),
        grid_spec=grid_spec,
        out_shape=jax.ShapeDtypeStruct((db, rows, dh), F32),
        compiler_params=_cparams(2),
    )(page_table.reshape(-1), q_rows, kv_new, g_new, *([cache_kv] * pps), *([sfx] * pps))


def _mla_dec_kernel(pt_ref, q_ref, latn_ref, wuv_ref, *rest, pps, n_new, lora, scale):
    lat_refs = rest[:pps]
    o_ref, m_ref, l_ref, acc_ref, pad_ref = rest[pps:]
    c = pl.program_id(1)
    q = q_ref[...]
    rows = q.shape[0]

    @pl.when(c == 0)
    def _():
        _stage_new_rows(pad_ref, latn_ref)
        lat = pad_ref[...].astype(BF16)
        s = jnp.where(_new_token_mask(rows, n_new, False), _nt(q, lat) * scale, NEG)
        _softmax_update(m_ref, l_ref, acc_ref, s, lambda p: _dot(p.astype(BF16), lat[:, :lora]), True)

    lat = jnp.concatenate([lat_refs[r][...].astype(BF16) for r in range(pps)], axis=1)
    l, acc = _softmax_update(m_ref, l_ref, acc_ref, _dot(q, lat) * scale,
                             lambda p: _nt(p.astype(BF16), lat[:lora]), False)

    @pl.when(c == pl.num_programs(1) - 1)
    def _():
        o_ref[...] = _dot((acc / l).astype(BF16), wuv_ref[...])


def _mla_dec(layer, q_rows, lat_new, w_uv_all, cache_lat, page_table, n_new, lora, scale):
    db, rows, wq = q_rows.shape
    n_pages = page_table.shape[1]
    pps = min(PAGES_PER_STEP, n_pages)
    lat_block = cache_lat.shape[2:]
    wo = w_uv_all.shape[1]
    grid_spec = pltpu.PrefetchScalarGridSpec(
        num_scalar_prefetch=1,
        grid=(db, n_pages // pps),
        in_specs=[pl.BlockSpec((None, rows, wq), lambda b, c, pt: (b, 0, 0)),
                  pl.BlockSpec((None,) + lat_new.shape[1:], lambda b, c, pt: (b, 0, 0)),
                  pl.BlockSpec(w_uv_all.shape, lambda b, c, pt: (0, 0))]
        + _page_specs(lat_block, layer, n_pages, pps),
        out_specs=pl.BlockSpec((None, rows, wo), lambda b, c, pt: (b, 0, 0)),
        scratch_shapes=[pltpu.VMEM((rows, 1), F32), pltpu.VMEM((rows, 1), F32), pltpu.VMEM((rows, lora), F32),
                        pltpu.VMEM((lat_block[-1], lat_new.shape[2]), F32)])
    return pl.pallas_call(
        functools.partial(_mla_dec_kernel, pps=pps, n_new=n_new, lora=lora, scale=scale),
        grid_spec=grid_spec,
        out_shape=jax.ShapeDtypeStruct((db, rows, wo), F32),
        compiler_params=_cparams(2),
    )(page_table.reshape(-1), q_rows, lat_new, w_uv_all, *([cache_lat] * pps))


def _sb_dec_kernel(pt_ref, q_ref, kvn_ref, *rest, pps, n_new):
    kv_refs = rest[:pps]
    o_ref, a_ref, acc_ref, pad_ref = rest[pps:]
    c = pl.program_id(1)
    q = q_ref[...]
    rows = q.shape[0]
    msuf = _suffix_matrix(LANES, False)

    @pl.when(c == 0)
    def _():
        _stage_new_rows(pad_ref, kvn_ref)
        z, vs = _head_scores_rows(q, pad_ref)
        w, a = _sb_weights(z, jnp.zeros(a_ref.shape, F32), msuf, _new_token_mask(rows, n_new, True))
        a_ref[...] = a
        acc_ref[...] = _head_pv_rows(w, vs)

    z, vs = _head_scores(q, kv_refs)
    ls = _log_sigmoid(z)
    lfail = ls - z
    pages = [lfail[:, r * LANES:(r + 1) * LANES] for r in range(pps)]
    ct = _dot_f32_lhs2(jnp.concatenate(pages, axis=0), msuf)
    a = a_ref[...]
    cums = []
    for r in range(pps):
        cums.append(ct[r * rows:(r + 1) * rows] + a)
        a = a + _row_sum(pages[r])
    a_ref[...] = a
    acc = acc_ref[...] + _head_pv(jnp.exp(ls + jnp.concatenate(cums, axis=1)), vs)
    acc_ref[...] = acc

    @pl.when(c == pl.num_programs(1) - 1)
    def _():
        o_ref[...] = acc


def _sb_dec(layer, q_rows, kv_new, cache_kv, page_table, n_new):
    db, rows, dh = q_rows.shape
    n_pages = page_table.shape[1]
    pps = min(PAGES_PER_STEP, n_pages)
    kv_block = cache_kv.shape[2:]
    grid_spec = pltpu.PrefetchScalarGridSpec(
        num_scalar_prefetch=1,
        grid=(db, n_pages // pps),
        in_specs=[pl.BlockSpec((None, rows, dh), lambda b, c, pt: (b, 0, 0)),
                  pl.BlockSpec((None,) + kv_new.shape[1:], lambda b, c, pt: (b, 0, 0))]
        + _page_specs(kv_block, layer, n_pages, pps),
        out_specs=pl.BlockSpec((None, rows, dh), lambda b, c, pt: (b, 0, 0)),
        scratch_shapes=[pltpu.VMEM((rows, 1), F32), pltpu.VMEM((rows, dh), F32),
                        pltpu.VMEM((kv_block[-1], kv_new.shape[2]), F32)])
    return pl.pallas_call(
        functools.partial(_sb_dec_kernel, pps=pps, n_new=n_new),
        grid_spec=grid_spec,
        out_shape=jax.ShapeDtypeStruct((db, rows, dh), F32),
        compiler_params=_cparams(2),
    )(page_table.reshape(-1), q_rows, kv_new, *([cache_kv] * pps))


def _layer_norm(v, g, b):
    mu = jnp.mean(v, axis=-1, keepdims=True)
    var = jnp.mean(jnp.square(v - mu), axis=-1, keepdims=True)
    return (v - mu) * lax.rsqrt(var + LN_EPS) * g + b


def _merge_kernel(x_ref, mod_ref, ba_ref, bb_ref, bc_ref, bd_ref, wg_ref, wb_ref, wo_ref, g_ref, b_ref,
                  wr_ref, br_ref, cnt0_ref, x1_ref, h2_ref, ti_ref, tp_ref, rk_ref, cnt_ref, *, alpha):
    @pl.when(pl.program_id(0) == 0)
    def _():
        cnt_ref[...] = cnt0_ref[...]

    x = x_ref[...]
    tm, d = x.shape
    h = (x * (1.0 + mod_ref[1]) + mod_ref[0]).astype(BF16)
    y = None
    for n, br in enumerate((ba_ref, bb_ref, bc_ref, bd_ref)):
        gate = jax.nn.sigmoid(_dot(h, wg_ref[:, n * d:(n + 1) * d]))
        term = gate * _dot(br[...], wb_ref[n])
        y = term if y is None else y + term
    y = _dot(y.astype(BF16), wo_ref[...])
    x1 = _layer_norm(alpha * x + (1.0 + mod_ref[2]) * y, g_ref[...], b_ref[...])
    x1_ref[...] = x1
    h2 = (x1 * (1.0 + mod_ref[4]) + mod_ref[3]).astype(BF16)
    h2_ref[...] = h2
    logits = _dot(h2, wr_ref[...]) + br_ref[...]
    ne = logits.shape[1]
    lane = lax.broadcasted_iota(jnp.int32, logits.shape, 1).astype(F32)
    kcol = lax.broadcasted_iota(jnp.int32, (tm, TOP_K), 1)
    tv = jnp.zeros((tm, TOP_K), F32)
    ti = jnp.zeros((tm, TOP_K), F32)
    picks = []
    for k in range(TOP_K):
        m = jnp.max(logits, axis=-1, keepdims=True)
        idx = jnp.min(jnp.where(logits == m, lane, float(ne)), axis=-1, keepdims=True)
        pick = lane == idx
        picks.append(pick)
        tv = jnp.where(kcol == k, m, tv)
        ti = jnp.where(kcol == k, idx, ti)
        logits = jnp.where(pick, -jnp.inf, logits)
    e = jnp.exp(tv - tv[:, 0:1])
    ti_ref[...] = ti.astype(jnp.int32)
    tp_ref[...] = e / jnp.sum(e, axis=-1, keepdims=True)
    chosen = None
    for pick in picks:
        one = jnp.where(pick, 1.0, 0.0)
        chosen = one if chosen is None else chosen + one
    r = lax.broadcasted_iota(jnp.int32, (tm, tm), 0)
    c = lax.broadcasted_iota(jnp.int32, (tm, tm), 1)
    earlier = jnp.where(c < r, 1.0, 0.0).astype(BF16)
    base = cnt_ref[...] + _dot(earlier, chosen.astype(BF16))
    rk = jnp.zeros((tm, TOP_K), F32)
    for k, pick in enumerate(picks):
        rk = jnp.where(kcol == k, jnp.sum(jnp.where(pick, base, 0.0), axis=-1, keepdims=True), rk)
    rk_ref[...] = rk.astype(jnp.int32)
    cnt_ref[...] = cnt_ref[...] + jnp.sum(chosen, axis=0, keepdims=True)


def _merge(x2d, mod, branches, w_gate, w_branch, w_out, ln_g, ln_b, w_router, b_router, counts, tm, rows_per_seq, alpha):
    n, d = x2d.shape
    full = lambda a: pl.BlockSpec(a.shape, lambda i: (0,) * a.ndim)
    row = lambda w: pl.BlockSpec((tm, w), lambda i: (i, 0))
    return pl.pallas_call(
        functools.partial(_merge_kernel, alpha=alpha),
        grid=(n // tm,),
        in_specs=[row(d), _mod_spec(mod, tm, rows_per_seq)] + [row(b.shape[1]) for b in branches]
        + [full(w_gate), full(w_branch), full(w_out), full(ln_g), full(ln_b), full(w_router), full(b_router),
           full(counts)],
        out_specs=[row(d), row(d), row(TOP_K), row(TOP_K), row(TOP_K), full(counts)],
        out_shape=[jax.ShapeDtypeStruct((n, d), F32), jax.ShapeDtypeStruct((n, d), BF16),
                   jax.ShapeDtypeStruct((n, TOP_K), jnp.int32), jax.ShapeDtypeStruct((n, TOP_K), F32),
                   jax.ShapeDtypeStruct((n, TOP_K), jnp.int32), jax.ShapeDtypeStruct(counts.shape, F32)],
        compiler_params=_cparams(1),
    )(x2d, mod, *branches, w_gate, w_branch, w_out, ln_g, ln_b, w_router, b_router, counts)


def _expert_kernel(be_ref, nb_ref, x_ref, w1_ref, b1_ref, w2_ref, b2_ref, o_ref, w1b_ref):
    i = pl.program_id(0)

    @pl.when((i == 0) | (be_ref[i] != be_ref[jnp.maximum(i - 1, 0)]))
    def _():
        w1b_ref[...] = w1_ref[...].astype(BF16)

    @pl.when(i < nb_ref[0])
    def _():
        a = _dot(x_ref[...], w1b_ref[...]) + b1_ref[...]
        glu = jnp.minimum(a, SWIGLU_LIMIT)
        lin = jnp.clip(a, -SWIGLU_LIMIT, SWIGLU_LIMIT) + 1.0
        act = glu * jax.nn.sigmoid(SWIGLU_ALPHA * glu) * pltpu.roll(lin, a.shape[1] - 1, 1)
        o_ref[...] = _dot(act.astype(BF16), w2_ref[...]) + b2_ref[...]


def _experts(block_exp, n_used, xs, w1, b1, w2x, b2):
    n_slots, d = xs.shape
    f2 = w1.shape[2]
    nb = n_slots // MOE_ROWS
    blk = lambda i, be, nu: (jnp.minimum(i, nu[0] - 1), 0)
    wspec = lambda s: pl.BlockSpec((None,) + s, lambda i, be, nu: (be[jnp.minimum(i, nu[0] - 1)], 0, 0))
    grid_spec = pltpu.PrefetchScalarGridSpec(
        num_scalar_prefetch=2,
        grid=(nb,),
        in_specs=[pl.BlockSpec((MOE_ROWS, d), blk),
                  wspec((d, f2)), wspec((1, f2)), wspec((f2, d)), wspec((1, d))],
        out_specs=pl.BlockSpec((MOE_ROWS, d), blk),
        scratch_shapes=[pltpu.VMEM((d, f2), BF16)])
    return pl.pallas_call(
        _expert_kernel,
        grid_spec=grid_spec,
        out_shape=jax.ShapeDtypeStruct((n_slots, d), F32),
        compiler_params=_cparams(1),
    )(block_exp, n_used, xs, w1, b1, w2x, b2)


def _final_kernel(x_ref, mod_ref, *rest, alpha):
    y_refs, (p_ref, g_ref, b_ref, o_ref) = rest[:TOP_K], rest[TOP_K:]
    p = p_ref[...]
    f = None
    for k in range(TOP_K):
        term = p[:, k:k + 1] * y_refs[k][...]
        f = term if f is None else f + term
    o_ref[...] = _layer_norm(alpha * x_ref[...] + (1.0 + mod_ref[5]) * f, g_ref[...], b_ref[...])


def _final(x1, mod, ys, probs, ln_g, ln_b, tm, rows_per_seq, alpha):
    n, d = x1.shape
    full = lambda a: pl.BlockSpec(a.shape, lambda i: (0,) * a.ndim)
    return pl.pallas_call(
        functools.partial(_final_kernel, alpha=alpha),
        grid=(n // tm,),
        in_specs=[pl.BlockSpec((tm, d), lambda i: (i, 0)), _mod_spec(mod, tm, rows_per_seq)]
        + [pl.BlockSpec((tm, d), lambda i: (i, 0))] * TOP_K
        + [pl.BlockSpec((tm, TOP_K), lambda i: (i, 0)), full(ln_g), full(ln_b)],
        out_specs=pl.BlockSpec((tm, d), lambda i: (i, 0)),
        out_shape=jax.ShapeDtypeStruct((n, d), F32),
        compiler_params=_cparams(1),
    )(x1, mod, *ys, probs, ln_g, ln_b)


def _rope_tables(pos, rope):
    half = rope // 2
    inv = ROPE_BASE ** (-jnp.arange(half, dtype=F32) / half)
    ang = pos.astype(F32)[:, None] * inv[None, :]
    cos, sin = jnp.cos(ang), jnp.sin(ang)
    reps = LANES // rope
    return (jnp.tile(jnp.concatenate([cos, cos], -1), (1, reps)),
            jnp.tile(jnp.concatenate([-sin, sin], -1), (1, reps)))


def _swap_halves(w):
    half = w.shape[-1] // 2
    return jnp.concatenate([w[..., half:], w[..., :half]], axis=-1)


def _layer_weights(l, p, dims):
    d, dp, hd, q_lora, kv_lora, rope = dims
    w_in = p['w_in'][l]
    widths = (dp, 3 * hd, N_HEADS, q_lora, kv_lora, rope, 3 * hd, N_BRANCH * d)
    offs = [0]
    for w in widths:
        offs.append(offs[-1] + w)
    cols = [w_in[:, offs[i]:offs[i + 1]] for i in range(len(widths))]
    w_pool, w_fox, w_f, w_cq, w_ckv, w_kr, w_sb, w_gate = cols
    small = jnp.concatenate([w_kr, _swap_halves(w_kr), w_f, jnp.zeros((d, LANES - 2 * rope - N_HEADS), F32)], axis=1)
    w_a = jnp.concatenate([w_pool, w_fox, w_cq, w_ckv, w_sb, small], axis=1).astype(BF16)
    eye = jnp.eye(N_HEADS, dtype=F32)
    w_uq = p['w_mla_uq'][l]
    nope = w_uq.shape[-1] - rope
    pad = lambda w: jnp.pad(w, ((0, 0), (0, 0), (0, LANES - rope))).reshape(q_lora, N_HEADS * LANES)
    w_rope = w_uq[:, :, nope:]
    lw = {
        'w_a': w_a,
        'w_gate': w_gate.astype(BF16),
        'w_pool_bd': jnp.einsum('gcd,gh->gchd', p['w_pool_mix'][l], jnp.eye(len(POOL_WINDOWS), dtype=F32)
                                ).reshape(dp, dp).astype(BF16),
        'pool_scale': p['pool_scale'][l].reshape(1, dp),
        'fox_bias': jnp.zeros((1, LANES), F32).at[0, 2 * rope:2 * rope + N_HEADS].set(p['b_fox_forget'][l]),
        'g_q': p['mla_q_norm'][l].reshape(1, q_lora),
        'g_kv': p['mla_kv_norm'][l].reshape(1, kv_lora),
        'w_nope': w_uq[:, :, :nope].reshape(q_lora, N_HEADS * nope).astype(BF16),
        'w_rope': pad(w_rope).astype(BF16),
        'w_rope_sw': pad(_swap_halves(w_rope)).astype(BF16),
        'w_uk_bd': jnp.einsum('chn,hg->hngc', p['w_mla_uk'][l], eye).reshape(N_HEADS * nope, N_HEADS * kv_lora).astype(BF16),
        'w_uv_bd': jnp.einsum('chv,hg->hcgv', p['w_mla_uv'][l], eye).reshape(N_HEADS * kv_lora, -1).astype(BF16),
        'w_uv_all': p['w_mla_uv'][l].reshape(kv_lora, -1).astype(BF16),
        'w_branch': p['w_branch'][l].astype(BF16),
        'w_out': p['w_out'][l].astype(BF16),
        'ln1_g': p['ln1_g'][l].reshape(1, d), 'ln1_b': p['ln1_b'][l].reshape(1, d),
        'ln2_g': p['ln2_g'][l].reshape(1, d), 'ln2_b': p['ln2_b'][l].reshape(1, d),
        'b_router': p['b_router'][l].reshape(1, -1),
        'w1': p['w_exp1'][l], 'b1': p['b_exp1'][l][:, None, :],
        'w2x': lax.pad(p['w_exp2'][l].astype(BF16), jnp.zeros((), BF16), ((0, 0, 0), (0, 1, 1), (0, 0, 0))),
        'b2': p['b_exp2'][l][:, None, :],
        'w_router': p['w_router'][l].astype(BF16),
    }
    lw['widths'] = (dp, 3 * hd, q_lora + kv_lora, 3 * hd, LANES)
    lw['nope'] = nope
    return lw


def _routing(top_i, rank, counts):
    n = top_i.shape[0]
    nk = n * TOP_K
    n_experts = counts.shape[-1]
    sizes = counts.reshape(n_experts).astype(jnp.int32)
    padded = (sizes + MOE_ROWS - 1) // MOE_ROWS * MOE_ROWS
    ends = jnp.cumsum(padded)
    starts = ends - padded
    first = jnp.cumsum(sizes) - sizes
    dest = starts[top_i] + rank
    n_blocks = -(-(nk + n_experts * (MOE_ROWS - 1)) // MOE_ROWS)
    blk_start = jnp.arange(n_blocks, dtype=jnp.int32) * MOE_ROWS
    block_exp = jnp.minimum(jnp.sum(ends[None, :] <= blk_start[:, None], axis=1), n_experts - 1).astype(jnp.int32)
    order = jnp.argsort(top_i.reshape(nk))
    slot = jnp.arange(n_blocks * MOE_ROWS, dtype=jnp.int32)
    e_slot = jnp.repeat(block_exp, MOE_ROWS)
    r = slot - starts[e_slot]
    pair = order[jnp.clip(first[e_slot] + r, 0, nk - 1)]
    slot_tok = jnp.where(r < sizes[e_slot], pair // TOP_K, 0).astype(jnp.int32)
    n_used = (ends[-1] // MOE_ROWS).astype(jnp.int32).reshape(1)
    return slot_tok, dest, block_exp, n_used


def _head_rows(q, scale):
    db, dt, hd = q.shape
    dh = hd // N_HEADS
    q4 = jnp.transpose((q * scale).reshape(db, dt, N_HEADS, dh), (0, 2, 1, 3))
    q4 = jnp.pad(q4, ((0, 0), (0, 0), (0, DEC_ROWS - dt), (0, 0)))
    return q4.reshape(db, N_HEADS * DEC_ROWS, dh).astype(BF16)


def _head_out(o, dt):
    db, _, dv = o.shape
    o4 = jnp.transpose(o.reshape(db, N_HEADS, DEC_ROWS, dv)[:, :, :dt], (0, 2, 1, 3))
    return o4.reshape(db * dt, N_HEADS * dv).astype(BF16)


def _diag_heads(o, dt):
    db, _, hv = o.shape
    dv = hv // N_HEADS
    o5 = o.reshape(db, N_HEADS, dt, N_HEADS, dv)
    d = jnp.stack([o5[:, h, :, h, :] for h in range(N_HEADS)], axis=2)
    return d.reshape(db * dt, hv).astype(BF16)


def kernel(x_prompt, x_sample, c_prompt, c_sample, cache_fox_kv, cache_fox_logf, cache_mla, cache_sb_kv, state_pool, page_table, w_ada, b_ada, w_in, b_fox_forget, w_pool_mix, pool_scale, mla_q_norm, w_mla_uq, mla_kv_norm, w_mla_uk, w_mla_uv, w_branch, w_out, ln1_g, ln1_b, w_router, b_router, w_exp1, b_exp1, w_exp2, b_exp2, ln2_g, ln2_b):
    p = dict(w_in=w_in, b_fox_forget=b_fox_forget, w_pool_mix=w_pool_mix, pool_scale=pool_scale,
             mla_q_norm=mla_q_norm, w_mla_uq=w_mla_uq, mla_kv_norm=mla_kv_norm, w_mla_uk=w_mla_uk,
             w_mla_uv=w_mla_uv, w_branch=w_branch, w_out=w_out, ln1_g=ln1_g, ln1_b=ln1_b, w_router=w_router,
             b_router=b_router, w_exp1=w_exp1, b_exp1=b_exp1, w_exp2=w_exp2, b_exp2=b_exp2, ln2_g=ln2_g, ln2_b=ln2_b)
    b, t, d = x_prompt.shape
    db, dt, _ = x_sample.shape
    depth = w_ada.shape[0]
    n_pages = page_table.shape[1]
    page = cache_fox_kv.shape[2]
    past = n_pages * page
    dp = state_pool.shape[-1]
    hd = cache_fox_kv.shape[-1] * cache_fox_kv.shape[-2]
    dh = hd // N_HEADS
    q_lora = mla_q_norm.shape[1]
    kv_lora = mla_kv_norm.shape[1]
    rope = cache_mla.shape[-1] - kv_lora
    n_experts = w_router.shape[-1]
    alpha = (2 * depth) ** 0.25
    dims = (d, dp, hd, q_lora, kv_lora, rope)
    np_rows, ns_rows = b * t, db * dt
    tm = min(ROW_TILE, t)
    tile = min(ATT_TILE, t)
    sb_tile = min(SB_TILE, t)
    nq = t // tile

    mods = _ada(jnp.concatenate([c_prompt, c_sample], axis=0), w_ada, b_ada)
    cos_p, sin_p = _rope_tables(jnp.arange(t), rope)
    cos_s, sin_s = (jnp.tile(a, (db, 1)) for a in _rope_tables(past + jnp.arange(dt), rope))
    fox_kv_t = jnp.transpose(cache_fox_kv, (0, 1, 3, 4, 5, 2))
    sb_kv_t = jnp.transpose(cache_sb_kv, (0, 1, 3, 4, 5, 2))
    mla_t = jnp.transpose(cache_mla, (0, 1, 3, 2))
    logf_t = jnp.transpose(cache_fox_logf, (0, 1, 3, 2))
    sfx = _fox_sfx(logf_t.reshape(-1, page)).reshape(logf_t.shape[:3] + (2 * page,))
    zero_counts = jnp.zeros((1, n_experts), F32)

    xp = x_prompt.reshape(np_rows, d)
    xs = x_sample.reshape(ns_rows, d)
    outs = {k: [] for k in ('fox_p', 'fox_s', 'lf_p', 'lf_s', 'mla_p', 'mla_s', 'sb_p', 'sb_s', 'pool_p', 'pool_s')}
    for l in range(depth):
        lw = _layer_weights(l, p, dims)
        mod_p = jnp.transpose(mods[l, :, :b], (1, 0, 2))[:, :, None, :]
        mod_s = jnp.repeat(mods[l, :, b:], dt, axis=1)[None]
        mla_scale = (lw['nope'] + rope) ** -0.5

        u_pool, u_fox, u_mla, u_sb, u_small = _inproj(xp, mod_p, lw['w_a'], lw['widths'], tm, t)
        br_a, pool_new = _pool(u_pool.reshape(b, t, dp), jnp.zeros((b, POOL_BUF, dp), F32),
                               lw['w_pool_bd'], lw['pool_scale'], 0)
        lf, cum = _fox_prep(u_small.reshape(b, t, LANES), lw['fox_bias'], 2 * rope)
        c_s = jnp.transpose(cum, (0, 2, 1)).reshape(b, N_HEADS, nq, 1, tile)
        br_b = _fox_attn(u_fox.reshape(b, t, 3 * hd), cum, c_s, tile)
        q_cat, lat_pad, lat_new = _mla_prep(u_mla, u_small, cos_p, sin_p, lw['g_q'], lw['g_kv'], lw['w_nope'],
                                            lw['w_rope'], lw['w_rope_sw'], lw['w_uk_bd'], tm, rope)
        br_c = _mla_attn(q_cat.reshape(b, t, -1), lat_pad.reshape(b, t, -1), lw['w_uv_bd'], tile, kv_lora, mla_scale)
        br_d = _sb_attn(u_sb.reshape(b, t, 3 * hd), sb_tile)
        branches = [br_a.reshape(np_rows, dp), br_b.reshape(np_rows, hd), br_c.reshape(np_rows, hd),
                    br_d.reshape(np_rows, hd)]
        x1_p, h2_p, ti_p, tp_p, rk_p, counts = _merge(
            xp, mod_p, branches, lw['w_gate'], lw['w_branch'], lw['w_out'], lw['ln1_g'], lw['ln1_b'],
            lw['w_router'], lw['b_router'], zero_counts, tm, t, alpha)
        outs['fox_p'].append(u_fox[:, hd:].reshape(b, t, 2, N_HEADS, dh))
        outs['lf_p'].append(lf)
        outs['mla_p'].append(lat_new.reshape(b, t, kv_lora + rope))
        outs['sb_p'].append(u_sb[:, hd:].reshape(b, t, 2, N_HEADS, dh))
        outs['pool_p'].append(pool_new)

        u_pool, u_fox, u_mla, u_sb, u_small = _inproj(xs, mod_s, lw['w_a'], lw['widths'], ns_rows, dt)
        br_a, pool_new = _pool(u_pool.reshape(db, dt, dp), state_pool[l], lw['w_pool_bd'], lw['pool_scale'], past)
        lf, cum = _fox_prep(u_small.reshape(db, dt, LANES), lw['fox_bias'], 2 * rope)
        u_fox3 = u_fox.reshape(db, dt, 3 * hd)
        g_new = jnp.broadcast_to(-jnp.transpose(cum, (0, 2, 1))[:, :, None, :], (db, N_HEADS, DEC_ROWS, dt))
        g_new = jnp.pad(g_new.reshape(db, N_HEADS * DEC_ROWS, dt), ((0, 0), (0, 0), (0, LANES - dt)))
        o_fox = _fox_dec(l, _head_rows(u_fox3[:, :, :hd], dh ** -0.5), u_fox3[:, :, hd:], g_new,
                         fox_kv_t, sfx, page_table, dt)
        q_cat, lat_pad, lat_new = _mla_prep(u_mla, u_small, cos_s, sin_s, lw['g_q'], lw['g_kv'], lw['w_nope'],
                                            lw['w_rope'], lw['w_rope_sw'], lw['w_uk_bd'], ns_rows, rope)
        wq = kv_lora + LANES
        q_rows = q_cat.reshape(db, dt, N_HEADS, wq)[..., :kv_lora + rope]
        q_rows = jnp.transpose(q_rows, (0, 2, 1, 3)).reshape(db, N_HEADS * dt, kv_lora + rope)
        o_mla = _mla_dec(l, q_rows, lat_new.reshape(db, dt, -1), lw['w_uv_all'], mla_t, page_table, dt, kv_lora,
                         mla_scale)
        u_sb3 = u_sb.reshape(db, dt, 3 * hd)
        o_sb = _sb_dec(l, _head_rows(u_sb3[:, :, :hd], dh ** -0.5), u_sb3[:, :, hd:],
                       sb_kv_t, page_table, dt)
        branches = [br_a.reshape(ns_rows, dp), _head_out(o_fox, dt), _diag_heads(o_mla, dt), _head_out(o_sb, dt)]
        x1_s, h2_s, ti_s, tp_s, rk_s, counts = _merge(
            xs, mod_s, branches, lw['w_gate'], lw['w_branch'], lw['w_out'], lw['ln1_g'], lw['ln1_b'],
            lw['w_router'], lw['b_router'], counts, ns_rows, dt, alpha)
        outs['fox_s'].append(u_fox3[:, :, hd:].reshape(db, dt, 2, N_HEADS, dh))
        outs['lf_s'].append(lf)
        outs['mla_s'].append(lat_new.reshape(db, dt, kv_lora + rope))
        outs['sb_s'].append(u_sb3[:, :, hd:].reshape(db, dt, 2, N_HEADS, dh))
        outs['pool_s'].append(pool_new)

        h2 = jnp.concatenate([h2_p, h2_s], axis=0)
        slot_tok, dest, block_exp, n_used = _routing(jnp.concatenate([ti_p, ti_s], axis=0),
                                                     jnp.concatenate([rk_p, rk_s], axis=0), counts)
        y_slots = _experts(block_exp, n_used, h2[slot_tok], lw['w1'], lw['b1'], lw['w2x'], lw['b2'])
        y_p = [y_slots[dest[:np_rows, k]] for k in range(TOP_K)]
        y_s = [y_slots[dest[np_rows:, k]] for k in range(TOP_K)]
        xp = _final(x1_p, mod_p, y_p, tp_p, lw['ln2_g'], lw['ln2_b'], tm, t, alpha)
        xs = _final(x1_s, mod_s, y_s, tp_s, lw['ln2_g'], lw['ln2_b'], ns_rows, dt, alpha)

    st = lambda k: jnp.stack(outs[k])
    return (xp.reshape(b, t, d), xs.reshape(db, dt, d), st('fox_p'), st('fox_s'), st('lf_p'), st('lf_s'),
            st('mla_p'), st('mla_s'), st('sb_p'), st('sb_s'), st('pool_p'), st('pool_s'))
```

```python
import functools

import jax
import jax.numpy as jnp
from jax import lax
from jax.experimental import pallas as pl
from jax.experimental.pallas import tpu as pltpu

F32 = jnp.float32
BF16 = jnp.bfloat16

POOL_WINDOWS = (2, 4, 8, 16)
POOL_BUF = max(POOL_WINDOWS) - 1
N_HEADS = 4
N_BRANCH = 4
N_MOD = 6
TOP_K = 4
ROPE_BASE = 10000.0
SWIGLU_ALPHA = 1.702
SWIGLU_LIMIT = 7.0
LN_EPS = 1e-5
RMS_EPS = 1e-6

LANES = 128
SUBLANES = 8
VMEM_LIMIT = 56 * 1024 * 1024

NEG = -1e30

ROW_TILE = 512
ATT_TILE = 512
SB_TILE = 256
PAGES_PER_STEP = 16
MOE_ROWS = 256
DEC_ROWS = 16


def _cparams(n_axes):
    return pltpu.CompilerParams(dimension_semantics=("arbitrary",) * n_axes, vmem_limit_bytes=VMEM_LIMIT)


def _nt(a, b):
    return lax.dot_general(a, b, (((1,), (1,)), ((), ())), preferred_element_type=F32)


def _dot(a, b):
    return jnp.dot(a, b, preferred_element_type=F32)


def _split3(x):
    hi = x.astype(BF16)
    r = x - hi.astype(F32)
    mid = r.astype(BF16)
    lo = (r - mid.astype(F32)).astype(BF16)
    return hi, mid, lo


def _dot_f32_lhs(x, m01):
    hi, mid, lo = _split3(x)
    return _dot(hi, m01) + _dot(mid, m01) + _dot(lo, m01)


def _dot_f32_lhs2(x, m01):
    hi = x.astype(BF16)
    lo = (x - hi.astype(F32)).astype(BF16)
    return _dot(hi, m01) + _dot(lo, m01)


def _dot_f32_rhs(m01, x):
    hi, mid, lo = _split3(x)
    return _dot(m01, hi) + _dot(m01, mid) + _dot(m01, lo)


def _log_sigmoid(x):
    return jnp.minimum(x, 0.0) - jnp.log1p(jnp.exp(-jnp.abs(x)))


def _mod_spec(mod, tm, rows_per_seq):
    d = mod.shape[-1]
    if mod.shape[2] == 1:
        return pl.BlockSpec((None, N_MOD, 1, d), lambda i: ((i * tm) // rows_per_seq, 0, 0, 0))
    return pl.BlockSpec((None, N_MOD, tm, d), lambda i: (0, 0, i, 0))


def _ada_kernel(c_ref, w_ref, b_ref, o_ref):
    c = c_ref[...]
    s = c * jax.nn.sigmoid(c)
    o_ref[...] = _dot(s.astype(BF16), w_ref[...].astype(BF16)) + b_ref[...]


def _ada(c_all, w_ada, b_ada):
    depth, d, _ = w_ada.shape
    r = c_all.shape[0]
    b4 = b_ada.reshape(depth, N_MOD, 1, d)
    return pl.pallas_call(
        _ada_kernel,
        grid=(depth, N_MOD),
        in_specs=[pl.BlockSpec((r, d), lambda l, k: (0, 0)),
                  pl.BlockSpec((None, d, d), lambda l, k: (l, 0, k)),
                  pl.BlockSpec((None, None, 1, d), lambda l, k: (l, k, 0, 0))],
        out_specs=pl.BlockSpec((None, None, r, d), lambda l, k: (l, k, 0, 0)),
        out_shape=jax.ShapeDtypeStruct((depth, N_MOD, r, d), F32),
        compiler_params=_cparams(2),
    )(c_all, w_ada, b4)


def _win_layout_kernel(w_ref, wa_ref, wg_ref, *, src, dst, rope_src, rope_dst, gate_src):
    wa_ref[...] = jnp.zeros(wa_ref.shape, wa_ref.dtype)
    for (s0, s1), d0 in zip(src, dst):
        wa_ref[:, d0:d0 + (s1 - s0)] = w_ref[:, s0:s1].astype(wa_ref.dtype)
    half = (rope_src[1] - rope_src[0]) // 2
    wa_ref[:, rope_dst:rope_dst + half] = w_ref[:, rope_src[0] + half:rope_src[1]].astype(wa_ref.dtype)
    wa_ref[:, rope_dst + half:rope_dst + 2 * half] = w_ref[:, rope_src[0]:rope_src[0] + half].astype(wa_ref.dtype)
    wg_ref[...] = w_ref[:, gate_src:gate_src + wg_ref.shape[1]].astype(wg_ref.dtype)


def _win_layout(w_in_l, dims):
    d, dp, hd, q_lora, kv_lora, rope = dims
    widths = (dp, 3 * hd, N_HEADS, q_lora, kv_lora, rope, 3 * hd, N_BRANCH * d)
    offs = [0]
    for w in widths:
        offs.append(offs[-1] + w)
    pool, fox, f, cq, ckv, kr, sb, gate = [(offs[i], offs[i + 1]) for i in range(len(widths))]
    small = dp + 3 * hd + q_lora + kv_lora + 3 * hd
    src = (pool, fox, cq, ckv, sb, kr, f)
    dst = (0, dp, dp + 3 * hd, dp + 3 * hd + q_lora, dp + 3 * hd + q_lora + kv_lora, small, small + 2 * rope)
    rows = LANES
    return pl.pallas_call(
        functools.partial(_win_layout_kernel, src=src, dst=dst, rope_src=kr, rope_dst=small + rope, gate_src=gate[0]),
        grid=(d // rows,),
        in_specs=[pl.BlockSpec((rows, w_in_l.shape[1]), lambda i: (i, 0))],
        out_specs=[pl.BlockSpec((rows, small + LANES), lambda i: (i, 0)),
                   pl.BlockSpec((rows, N_BRANCH * d), lambda i: (i, 0))],
        out_shape=[jax.ShapeDtypeStruct((d, small + LANES), BF16), jax.ShapeDtypeStruct((d, N_BRANCH * d), BF16)],
        compiler_params=_cparams(1),
    )(w_in_l)


def _inproj_kernel(x_ref, mod_ref, w_ref, *out_refs, widths):
    h = x_ref[...] * (1.0 + mod_ref[1]) + mod_ref[0]
    u = _dot(h.astype(BF16), w_ref[...])
    off = 0
    for o_ref, w in zip(out_refs, widths):
        o_ref[...] = u[:, off:off + w]
        off += w


def _inproj(x2d, mod, w_a, widths, tm, rows_per_seq):
    n, d = x2d.shape
    return pl.pallas_call(
        functools.partial(_inproj_kernel, widths=widths),
        grid=(n // tm,),
        in_specs=[pl.BlockSpec((tm, d), lambda i: (i, 0)),
                  _mod_spec(mod, tm, rows_per_seq),
                  pl.BlockSpec(w_a.shape, lambda i: (0, 0))],
        out_specs=[pl.BlockSpec((tm, w), lambda i: (i, 0)) for w in widths],
        out_shape=[jax.ShapeDtypeStruct((n, w), F32) for w in widths],
        compiler_params=_cparams(1),
    )(x2d, mod, w_a)


def _pool_kernel(u_ref, pre_ref, w_ref, sc_ref, o_ref, new_ref, z_ref, *, t, pos0):
    total, dp = z_ref.shape
    base = POOL_BUF + 1
    z_ref[0:1, :] = jnp.zeros((1, dp), F32)
    z_ref[1:base, :] = pre_ref[...]
    z_ref[base:base + t, :] = u_ref[...]
    if total > base + t:
        z_ref[base + t:total, :] = jnp.zeros((total - base - t, dp), F32)
    z = z_ref[...]
    sums = []
    s = z
    for step in (1, 2, 4, 8):
        s = s + pltpu.roll(s, step, 0)
        sums.append(s)
    lane = lax.broadcasted_iota(jnp.int32, (t, dp), 1)
    group = dp // len(POOL_WINDOWS)
    win = sums[-1][base:base + t]
    wlen = jnp.full((t, dp), float(POOL_WINDOWS[-1]), F32)
    for g in range(len(POOL_WINDOWS) - 2, -1, -1):
        sel = lane < (g + 1) * group
        win = jnp.where(sel, sums[g][base:base + t], win)
        wlen = jnp.where(sel, float(POOL_WINDOWS[g]), wlen)
    pos = (pos0 + lax.broadcasted_iota(jnp.int32, (t, dp), 0)).astype(F32)
    cnt = jnp.minimum(wlen, pos + 1.0)
    pooled = win / cnt - u_ref[...]
    o_ref[...] = (_dot(pooled.astype(BF16), w_ref[...]) * sc_ref[...]).astype(o_ref.dtype)
    new_ref[...] = z_ref[t + 1:t + 1 + POOL_BUF, :]


def _pool(u_pool, prefix, w_bd, scale, pos0):
    b, t, dp = u_pool.shape
    total = -(-(POOL_BUF + 1 + t) // SUBLANES) * SUBLANES
    return pl.pallas_call(
        functools.partial(_pool_kernel, t=t, pos0=pos0),
        grid=(b,),
        in_specs=[pl.BlockSpec((None, t, dp), lambda i: (i, 0, 0)),
                  pl.BlockSpec((None, POOL_BUF, dp), lambda i: (i, 0, 0)),
                  pl.BlockSpec((dp, dp), lambda i: (0, 0)),
                  pl.BlockSpec((1, dp), lambda i: (0, 0))],
        out_specs=[pl.BlockSpec((None, t, dp), lambda i: (i, 0, 0)),
                   pl.BlockSpec((None, POOL_BUF, dp), lambda i: (i, 0, 0))],
        out_shape=[jax.ShapeDtypeStruct((b, t, dp), BF16),
                   jax.ShapeDtypeStruct((b, POOL_BUF, dp), F32)],
        scratch_shapes=[pltpu.VMEM((total, dp), F32)],
        compiler_params=_cparams(1),
    )(u_pool, prefix, w_bd, scale)


def _fox_prep_kernel(s_ref, b_ref, lf_ref, c_ref, *, t, col):
    x = s_ref[...] + b_ref[...]
    lf = _log_sigmoid(x)
    lf_ref[...] = lf[:, col:col + N_HEADS]
    if t % LANES == 0:
        r = lax.broadcasted_iota(jnp.int32, (LANES, LANES), 0)
        c = lax.broadcasted_iota(jnp.int32, (LANES, LANES), 1)
        tri = jnp.where(r >= c, 1.0, 0.0).astype(BF16)
        carry = jnp.zeros((1, LANES), F32)
        for ch in range(t // LANES):
            cs = _dot_f32_rhs(tri, lf[ch * LANES:(ch + 1) * LANES]) + carry
            c_ref[ch * LANES:(ch + 1) * LANES, :] = cs[:, col:col + N_HEADS]
            carry = cs[LANES - 1:LANES, :]
    else:
        acc = jnp.zeros((1, LANES), F32)
        for i in range(t):
            acc = acc + lf[i:i + 1]
            c_ref[i:i + 1, :] = acc[:, col:col + N_HEADS]


def _fox_prep(small, bias_row, col):
    b, t, w = small.shape
    return pl.pallas_call(
        functools.partial(_fox_prep_kernel, t=t, col=col),
        grid=(b,),
        in_specs=[pl.BlockSpec((None, t, w), lambda i: (i, 0, 0)),
                  pl.BlockSpec((1, w), lambda i: (0, 0))],
        out_specs=[pl.BlockSpec((None, t, N_HEADS), lambda i: (i, 0, 0))] * 2,
        out_shape=[jax.ShapeDtypeStruct((b, t, N_HEADS), F32)] * 2,
        compiler_params=_cparams(1),
    )(small, bias_row)


def _softmax_step(carry, s, v):
    m, l, acc = carry
    m_new = jnp.maximum(m, jnp.max(s, axis=-1, keepdims=True))
    p = jnp.exp(s - m_new)
    alpha = jnp.exp(m - m_new)
    l = alpha * l + jnp.sum(p, axis=-1, keepdims=True)
    acc = alpha * acc + _dot(p.astype(BF16), v)
    return m_new, l, acc


def _causal_mask(tq, tk, strict):
    r = lax.broadcasted_iota(jnp.int32, (tq, tk), 0)
    c = lax.broadcasted_iota(jnp.int32, (tq, tk), 1)
    return (c < r) if strict else (c <= r)


def _fox_attn_kernel(q_ref, k_ref, v_ref, ct_ref, cs_ref, o_ref, *, tile, dh):
    i = pl.program_id(1)
    scale = dh ** -0.5
    mask = _causal_mask(tile, tile, False)
    heads = [slice(h * dh, (h + 1) * dh) for h in range(N_HEADS)]
    qs = [(q_ref[:, sl] * scale).astype(BF16) for sl in heads]
    cts = [ct_ref[:, h:h + 1] for h in range(N_HEADS)]

    def step(j, carry, masked):
        rows = pl.ds(pl.multiple_of(j * tile, tile), tile)
        out = []
        for h, sl in enumerate(heads):
            s = _nt(qs[h], k_ref[rows, sl].astype(BF16)) + (cts[h] - cs_ref[h, j])
            if masked:
                s = jnp.where(mask, s, NEG)
            out.append(_softmax_step(carry[h], s, v_ref[rows, sl].astype(BF16)))
        return tuple(out)

    init = tuple((jnp.full((tile, 1), NEG, F32), jnp.zeros((tile, 1), F32), jnp.zeros((tile, dh), F32))
                 for _ in heads)
    carry = lax.fori_loop(0, i, lambda j, c: step(j, c, False), init)
    carry = step(i, carry, True)
    for sl, (m, l, acc) in zip(heads, carry):
        o_ref[:, sl] = (acc / l).astype(o_ref.dtype)


def _fox_attn(u_fox, c_t, c_s, tile):
    b, t, w = u_fox.shape
    hd = w // 3
    dh = hd // N_HEADS
    nq = t // tile
    return pl.pallas_call(
        functools.partial(_fox_attn_kernel, tile=tile, dh=dh),
        grid=(b, nq),
        in_specs=[pl.BlockSpec((None, tile, hd), lambda bi, i: (bi, i, 0)),
                  pl.BlockSpec((None, t, hd), lambda bi, i: (bi, 0, 1)),
                  pl.BlockSpec((None, t, hd), lambda bi, i: (bi, 0, 2)),
                  pl.BlockSpec((None, tile, N_HEADS), lambda bi, i: (bi, i, 0)),
                  pl.BlockSpec((None, N_HEADS, nq, 1, tile), lambda bi, i: (bi, 0, 0, 0, 0))],
        out_specs=pl.BlockSpec((None, tile, hd), lambda bi, i: (bi, i, 0)),
        out_shape=jax.ShapeDtypeStruct((b, t, hd), BF16),
        compiler_params=_cparams(2),
    )(u_fox, u_fox, u_fox, c_t, c_s)


def _mla_attn_kernel(q_ref, lat_ref, wuv_ref, o_ref, *, tile, lora, scale):
    i = pl.program_id(1)
    wq = q_ref.shape[1] // N_HEADS
    q = jnp.concatenate([q_ref[:, h * wq:(h + 1) * wq] for h in range(N_HEADS)], axis=0)
    rows_all = N_HEADS * tile
    r = lax.broadcasted_iota(jnp.int32, (rows_all, tile), 0) % tile
    c = lax.broadcasted_iota(jnp.int32, (rows_all, tile), 1)
    mask = c <= r

    def step(j, carry, masked):
        lat = lat_ref[pl.ds(pl.multiple_of(j * tile, tile), tile), :]
        s = _nt(q, lat) * scale
        if masked:
            s = jnp.where(mask, s, NEG)
        return _softmax_step(carry, s, lat[:, :lora])

    init = (jnp.full((rows_all, 1), NEG, F32), jnp.zeros((rows_all, 1), F32), jnp.zeros((rows_all, lora), F32))
    carry = lax.fori_loop(0, i, lambda j, c: step(j, c, False), init)
    m, l, acc = step(i, carry, True)
    o = (acc / l).astype(BF16)
    o = jnp.concatenate([o[h * tile:(h + 1) * tile] for h in range(N_HEADS)], axis=-1)
    o_ref[...] = _dot(o, wuv_ref[...]).astype(o_ref.dtype)


def _mla_attn(q_cat, lat_pad, w_uv_bd, tile, lora, scale):
    b, t, wq = q_cat.shape
    wl = lat_pad.shape[-1]
    wo = w_uv_bd.shape[1]
    return pl.pallas_call(
        functools.partial(_mla_attn_kernel, tile=tile, lora=lora, scale=scale),
        grid=(b, t // tile),
        in_specs=[pl.BlockSpec((None, tile, wq), lambda bi, i: (bi, i, 0)),
                  pl.BlockSpec((None, t, wl), lambda bi, i: (bi, 0, 0)),
                  pl.BlockSpec(w_uv_bd.shape, lambda bi, i: (0, 0))],
        out_specs=pl.BlockSpec((None, tile, wo), lambda bi, i: (bi, i, 0)),
        out_shape=jax.ShapeDtypeStruct((b, t, wo), BF16),
        compiler_params=_cparams(2),
    )(q_cat, lat_pad, w_uv_bd)


def _suffix_matrix(n, with_total):
    j = lax.broadcasted_iota(jnp.int32, (n, n), 0)
    s = lax.broadcasted_iota(jnp.int32, (n, n), 1)
    m = jnp.where(j > s, 1.0, 0.0).astype(BF16)
    if with_total:
        m = jnp.concatenate([m, jnp.ones((n, n), BF16)], axis=1)
    return m


def _row_sum(x):
    return jnp.sum(x, axis=-1, keepdims=True)


def _sb_weights(z, carry_a, msuf, valid):
    ls = _log_sigmoid(z)
    lfail = ls - z
    if valid is not None:
        lfail = jnp.where(valid, lfail, 0.0)
    w = jnp.exp(ls + _dot_f32_lhs2(lfail, msuf) + carry_a)
    if valid is not None:
        w = jnp.where(valid, w, 0.0)
    return w, carry_a + _row_sum(lfail)


def _sb_block(z, v, carry_a, acc, msuf, valid):
    n = z.shape[1]
    ls = _log_sigmoid(z)
    lfail = ls - z
    if valid is not None:
        lfail = jnp.where(valid, lfail, 0.0)
    ct = _dot_f32_lhs2(lfail, msuf)
    w = jnp.exp(ls + ct[:, :n] + carry_a)
    if valid is not None:
        w = jnp.where(valid, w, 0.0)
    return carry_a + ct[:, n:], acc + _dot(w.astype(BF16), v)


def _sb_attn_kernel(q_ref, k_ref, v_ref, o_ref, *, tile, dh):
    i = pl.program_id(1)
    scale = dh ** -0.5
    valid = _causal_mask(tile, tile, True)
    msuf = _suffix_matrix(tile, True)
    heads = [slice(h * dh, (h + 1) * dh) for h in range(N_HEADS)]
    qs = [(q_ref[:, sl] * scale).astype(BF16) for sl in heads]

    def step(j, carry, mask):
        rows = pl.ds(pl.multiple_of(j * tile, tile), tile)
        return tuple(_sb_block(_nt(qs[h], k_ref[rows, sl].astype(BF16)), v_ref[rows, sl].astype(BF16),
                               carry[h][0], carry[h][1], msuf, mask) for h, sl in enumerate(heads))

    init = tuple((jnp.zeros((tile, tile), F32), jnp.zeros((tile, dh), F32)) for _ in heads)
    carry = step(i, init, valid)
    carry = lax.fori_loop(0, i, lambda t, c: step(i - 1 - t, c, None), carry)
    for sl, (a, acc) in zip(heads, carry):
        o_ref[:, sl] = acc.astype(o_ref.dtype)


def _sb_attn(u_sb, tile):
    b, t, w = u_sb.shape
    hd = w // 3
    dh = hd // N_HEADS
    return pl.pallas_call(
        functools.partial(_sb_attn_kernel, tile=tile, dh=dh),
        grid=(b, t // tile),
        in_specs=[pl.BlockSpec((None, tile, hd), lambda bi, i: (bi, i, 0)),
                  pl.BlockSpec((None, t, hd), lambda bi, i: (bi, 0, 1)),
                  pl.BlockSpec((None, t, hd), lambda bi, i: (bi, 0, 2))],
        out_specs=pl.BlockSpec((None, tile, hd), lambda bi, i: (bi, i, 0)),
        out_shape=jax.ShapeDtypeStruct((b, t, hd), BF16),
        compiler_params=_cparams(2),
    )(u_sb, u_sb, u_sb)


def _rms(x, g):
    return x * lax.rsqrt(jnp.mean(jnp.square(x), axis=-1, keepdims=True) + RMS_EPS) * g


def _mla_prep_kernel(u_ref, s_ref, cos_ref, sin_ref, gq_ref, gkv_ref, wn_ref, wr_ref, wrs_ref, wuk_ref,
                     qcat_ref, latpad_ref, lat_ref, *, q_lora, kv_lora, rope):
    cos = cos_ref[...]
    sin = sin_ref[...]
    cq = _rms(u_ref[:, :q_lora], gq_ref[...]).astype(BF16)
    q_nope = _dot(cq, wn_ref[...])
    q_rope = _dot(cq, wr_ref[...])
    q_rope_sw = _dot(cq, wrs_ref[...])
    q_abs = _dot(q_nope.astype(BF16), wuk_ref[...])
    pieces = []
    for h in range(N_HEADS):
        blk = slice(h * LANES, (h + 1) * LANES)
        pieces.append(q_abs[:, h * kv_lora:(h + 1) * kv_lora])
        pieces.append(q_rope[:, blk] * cos + q_rope_sw[:, blk] * sin)
    qcat_ref[...] = jnp.concatenate(pieces, axis=-1).astype(qcat_ref.dtype)
    ckv = _rms(u_ref[:, q_lora:q_lora + kv_lora], gkv_ref[...])
    small = s_ref[...]
    kr = small * cos + pltpu.roll(small, LANES - rope, 1) * sin
    lane = lax.broadcasted_iota(jnp.int32, kr.shape, 1)
    kr = jnp.where(lane < rope, kr, 0.0)
    latpad_ref[...] = jnp.concatenate([ckv, kr], axis=-1).astype(latpad_ref.dtype)
    lat_ref[...] = jnp.concatenate([ckv, kr[:, :rope]], axis=-1)


def _mla_prep(u_mla, small, cos, sin, g_q, g_kv, w_nope, w_rope, w_rope_sw, w_uk_bd, tm, rope):
    n, wm = u_mla.shape
    q_lora = g_q.shape[1]
    kv_lora = g_kv.shape[1]
    ncs = cos.shape[0] // tm
    wq = N_HEADS * (kv_lora + LANES)
    full = lambda a: pl.BlockSpec(a.shape, lambda i: (0, 0))
    return pl.pallas_call(
        functools.partial(_mla_prep_kernel, q_lora=q_lora, kv_lora=kv_lora, rope=rope),
        grid=(n // tm,),
        in_specs=[pl.BlockSpec((tm, wm), lambda i: (i, 0)),
                  pl.BlockSpec((tm, LANES), lambda i: (i, 0)),
                  pl.BlockSpec((tm, LANES), lambda i: (i % ncs, 0)),
                  pl.BlockSpec((tm, LANES), lambda i: (i % ncs, 0)),
                  full(g_q), full(g_kv), full(w_nope), full(w_rope), full(w_rope_sw), full(w_uk_bd)],
        out_specs=[pl.BlockSpec((tm, wq), lambda i: (i, 0)),
                   pl.BlockSpec((tm, kv_lora + LANES), lambda i: (i, 0)),
                   pl.BlockSpec((tm, kv_lora + rope), lambda i: (i, 0))],
        out_shape=[jax.ShapeDtypeStruct((n, wq), BF16),
                   jax.ShapeDtypeStruct((n, kv_lora + LANES), BF16),
                   jax.ShapeDtypeStruct((n, kv_lora + rope), F32)],
        compiler_params=_cparams(1),
    )(u_mla, small, cos, sin, g_q, g_kv, w_nope, w_rope, w_rope_sw, w_uk_bd)


def _page_specs(block, layer, n_pages, pps):
    zeros = (0,) * len(block)

    def spec(r):
        return pl.BlockSpec((None, None) + block,
                            lambda b, c, pt, r=r: (layer, pt[b * n_pages + (n_pages - 1 - (c * pps + r))]) + zeros)
    return [spec(r) for r in range(pps)]


def _new_token_mask(rows, n_new, strict):
    r = lax.broadcasted_iota(jnp.int32, (rows, LANES), 0) % (rows // N_HEADS)
    c = lax.broadcasted_iota(jnp.int32, (rows, LANES), 1)
    limit = jnp.minimum(r if strict else r + 1, n_new)
    return c < limit


def _head_scores(q, kv_refs):
    qr = q.shape[0] // N_HEADS
    ss, vs = [], []
    for h in range(N_HEADS):
        kt = jnp.concatenate([ref[0, h].astype(BF16) for ref in kv_refs], axis=1)
        ss.append(_dot(q[h * qr:(h + 1) * qr], kt))
        vs.append(jnp.concatenate([ref[1, h].astype(BF16) for ref in kv_refs], axis=1))
    return jnp.concatenate(ss, axis=0), vs


def _head_pv(p, vs):
    qr = p.shape[0] // N_HEADS
    return jnp.concatenate([_nt(p[h * qr:(h + 1) * qr].astype(BF16), vs[h]) for h in range(N_HEADS)], axis=0)


def _stage_new_rows(pad_ref, new_ref):
    pad_ref[...] = jnp.zeros(pad_ref.shape, F32)
    pad_ref[0:new_ref.shape[0], :] = new_ref[...]


def _head_scores_rows(q, pad_ref):
    qr = q.shape[0] // N_HEADS
    hd = pad_ref.shape[1] // 2
    dh = hd // N_HEADS
    ss = [_nt(q[h * qr:(h + 1) * qr], pad_ref[:, h * dh:(h + 1) * dh].astype(BF16)) for h in range(N_HEADS)]
    vs = [pad_ref[:, hd + h * dh:hd + (h + 1) * dh].astype(BF16) for h in range(N_HEADS)]
    return jnp.concatenate(ss, axis=0), vs


def _head_pv_rows(p, vs):
    qr = p.shape[0] // N_HEADS
    return jnp.concatenate([_dot(p[h * qr:(h + 1) * qr].astype(BF16), vs[h]) for h in range(N_HEADS)], axis=0)


def _fox_sfx_kernel(x_ref, m_ref, o_ref):
    o_ref[...] = _dot_f32_lhs(x_ref[...], m_ref[...])


def _fox_sfx(logf_rows):
    n, page = logf_rows.shape
    tr = 2048 if n % 2048 == 0 else n
    m = _suffix_matrix(page, True)
    return pl.pallas_call(
        _fox_sfx_kernel,
        grid=(n // tr,),
        in_specs=[pl.BlockSpec((tr, page), lambda i: (i, 0)), pl.BlockSpec(m.shape, lambda i: (0, 0))],
        out_specs=pl.BlockSpec((tr, 2 * page), lambda i: (i, 0)),
        out_shape=jax.ShapeDtypeStruct((n, 2 * page), F32),
        compiler_params=_cparams(1),
    )(logf_rows, m)


def _softmax_update(m_ref, l_ref, acc_ref, s, pv, first):
    smax = jnp.max(s, axis=-1, keepdims=True)
    if first:
        m_new = smax
    else:
        m_old = m_ref[...]
        m_new = jnp.maximum(m_old, smax)
        alpha = jnp.exp(m_old - m_new)
    p = jnp.exp(s - m_new)
    l = jnp.sum(p, axis=-1, keepdims=True)
    acc = pv(p)
    if not first:
        l = l + alpha * l_ref[...]
        acc = acc + alpha * acc_ref[...]
    m_ref[...] = m_new
    l_ref[...] = l
    acc_ref[...] = acc
    return l, acc


def _fox_dec_kernel(pt_ref, q_ref, kvn_ref, gn_ref, *rest, pps, n_new):
    kv_refs, sfx_refs = rest[:pps], rest[pps:2 * pps]
    o_ref, m_ref, l_ref, acc_ref, car_ref, pad_ref = rest[2 * pps:]
    c = pl.program_id(1)
    q = q_ref[...]
    rows = q.shape[0]
    qr = rows // N_HEADS

    @pl.when(c == 0)
    def _():
        _stage_new_rows(pad_ref, kvn_ref)
        s, vs = _head_scores_rows(q, pad_ref)
        s = jnp.where(_new_token_mask(rows, n_new, False), s + gn_ref[...], NEG)
        _softmax_update(m_ref, l_ref, acc_ref, s, lambda p: _head_pv_rows(p, vs), True)
        car_ref[...] = jnp.zeros(car_ref.shape, F32)

    car = car_ref[0:N_HEADS, :]
    biases = []
    for r in range(pps):
        blk = sfx_refs[r][...]
        biases.append(blk[:, :LANES] + car)
        car = car + blk[:, LANES:]
    car_ref[0:N_HEADS, :] = car
    bias = jnp.concatenate(biases, axis=1)
    s, vs = _head_scores(q, kv_refs)
    s = s + jnp.concatenate([jnp.broadcast_to(bias[h:h + 1, :], (qr, bias.shape[1])) for h in range(N_HEADS)], axis=0)
    l, acc = _softmax_update(m_ref, l_ref, acc_ref, s, lambda p: _head_pv(p, vs), False)

    @pl.when(c == pl.num_programs(1) - 1)
    def _():
        o_ref[...] = acc / l


def _fox_dec(layer, q_rows, kv_new, g_new, cache_kv, sfx, page_table, n_new):
    db, rows, dh = q_rows.shape
    n_pages = page_table.shape[1]
    pps = min(PAGES_PER_STEP, n_pages)
    kv_block = cache_kv.shape[2:]
    sfx_block = sfx.shape[2:]
    grid_spec = pltpu.PrefetchScalarGridSpec(
        num_scalar_prefetch=1,
        grid=(db, n_pages // pps),
        in_specs=[pl.BlockSpec((None, rows, dh), lambda b, c, pt: (b, 0, 0)),
                  pl.BlockSpec((None,) + kv_new.shape[1:], lambda b, c, pt: (b, 0, 0)),
                  pl.BlockSpec((None, rows, LANES), lambda b, c, pt: (b, 0, 0))]
        + _page_specs(kv_block, layer, n_pages, pps) + _page_specs(sfx_block, layer, n_pages, pps),
        out_specs=pl.BlockSpec((None, rows, dh), lambda b, c, pt: (b, 0, 0)),
        scratch_shapes=[pltpu.VMEM((rows, 1), F32), pltpu.VMEM((rows, 1), F32), pltpu.VMEM((rows, dh), F32),
                        pltpu.VMEM((SUBLANES, LANES), F32), pltpu.VMEM((kv_block[-1], kv_new.shape[2]), F32)])
    return pl.pallas_call(
        functools.partial(_fox_dec_kernel, pps=pps, n_new=n_new),
        grid_spec=grid_spec,
        out_shape=jax.ShapeDtypeStruct((db, rows, dh), F32),
        compiler_params=_cparams(2),
    )(page_table.reshape(-1), q_rows, kv_new, g_new, *([cache_kv] * pps), *([sfx] * pps))


def _mla_dec_kernel(pt_ref, q_ref, latn_ref, wuv_ref, *rest, pps, n_new, lora, scale):
    lat_refs = rest[:pps]
    o_ref, m_ref, l_ref, acc_ref, pad_ref = rest[pps:]
    c = pl.program_id(1)
    q = q_ref[...]
    rows = q.shape[0]

    @pl.when(c == 0)
    def _():
        _stage_new_rows(pad_ref, latn_ref)
        lat = pad_ref[...].astype(BF16)
        s = jnp.where(_new_token_mask(rows, n_new, False), _nt(q, lat) * scale, NEG)
        _softmax_update(m_ref, l_ref, acc_ref, s, lambda p: _dot(p.astype(BF16), lat[:, :lora]), True)

    lat = jnp.concatenate([lat_refs[r][...].astype(BF16) for r in range(pps)], axis=1)
    l, acc = _softmax_update(m_ref, l_ref, acc_ref, _dot(q, lat) * scale,
                             lambda p: _nt(p.astype(BF16), lat[:lora]), False)

    @pl.when(c == pl.num_programs(1) - 1)
    def _():
        o_ref[...] = _dot((acc / l).astype(BF16), wuv_ref[...])


def _mla_dec(layer, q_rows, lat_new, w_uv_all, cache_lat, page_table, n_new, lora, scale):
    db, rows, wq = q_rows.shape
    n_pages = page_table.shape[1]
    pps = min(PAGES_PER_STEP, n_pages)
    lat_block = cache_lat.shape[2:]
    wo = w_uv_all.shape[1]
    grid_spec = pltpu.PrefetchScalarGridSpec(
        num_scalar_prefetch=1,
        grid=(db, n_pages // pps),
        in_specs=[pl.BlockSpec((None, rows, wq), lambda b, c, pt: (b, 0, 0)),
                  pl.BlockSpec((None,) + lat_new.shape[1:], lambda b, c, pt: (b, 0, 0)),
                  pl.BlockSpec(w_uv_all.shape, lambda b, c, pt: (0, 0))]
        + _page_specs(lat_block, layer, n_pages, pps),
        out_specs=pl.BlockSpec((None, rows, wo), lambda b, c, pt: (b, 0, 0)),
        scratch_shapes=[pltpu.VMEM((rows, 1), F32), pltpu.VMEM((rows, 1), F32), pltpu.VMEM((rows, lora), F32),
                        pltpu.VMEM((lat_block[-1], lat_new.shape[2]), F32)])
    return pl.pallas_call(
        functools.partial(_mla_dec_kernel, pps=pps, n_new=n_new, lora=lora, scale=scale),
        grid_spec=grid_spec,
        out_shape=jax.ShapeDtypeStruct((db, rows, wo), F32),
        compiler_params=_cparams(2),
    )(page_table.reshape(-1), q_rows, lat_new, w_uv_all, *([cache_lat] * pps))


def _sb_dec_kernel(pt_ref, q_ref, kvn_ref, *rest, pps, n_new):
    kv_refs = rest[:pps]
    o_ref, a_ref, acc_ref, pad_ref = rest[pps:]
    c = pl.program_id(1)
    q = q_ref[...]
    rows = q.shape[0]
    msuf = _suffix_matrix(LANES, False)

    @pl.when(c == 0)
    def _():
        _stage_new_rows(pad_ref, kvn_ref)
        z, vs = _head_scores_rows(q, pad_ref)
        w, a = _sb_weights(z, jnp.zeros(a_ref.shape, F32), msuf, _new_token_mask(rows, n_new, True))
        a_ref[...] = a
        acc_ref[...] = _head_pv_rows(w, vs)

    z, vs = _head_scores(q, kv_refs)
    ls = _log_sigmoid(z)
    lfail = ls - z
    pages = [lfail[:, r * LANES:(r + 1) * LANES] for r in range(pps)]
    ct = _dot_f32_lhs2(jnp.concatenate(pages, axis=0), msuf)
    a = a_ref[...]
    cums = []
    for r in range(pps):
        cums.append(ct[r * rows:(r + 1) * rows] + a)
        a = a + _row_sum(pages[r])
    a_ref[...] = a
    acc = acc_ref[...] + _head_pv(jnp.exp(ls + jnp.concatenate(cums, axis=1)), vs)
    acc_ref[...] = acc

    @pl.when(c == pl.num_programs(1) - 1)
    def _():
        o_ref[...] = acc


def _sb_dec(layer, q_rows, kv_new, cache_kv, page_table, n_new):
    db, rows, dh = q_rows.shape
    n_pages = page_table.shape[1]
    pps = min(PAGES_PER_STEP, n_pages)
    kv_block = cache_kv.shape[2:]
    grid_spec = pltpu.PrefetchScalarGridSpec(
        num_scalar_prefetch=1,
        grid=(db, n_pages // pps),
        in_specs=[pl.BlockSpec((None, rows, dh), lambda b, c, pt: (b, 0, 0)),
                  pl.BlockSpec((None,) + kv_new.shape[1:], lambda b, c, pt: (b, 0, 0))]
        + _page_specs(kv_block, layer, n_pages, pps),
        out_specs=pl.BlockSpec((None, rows, dh), lambda b, c, pt: (b, 0, 0)),
        scratch_shapes=[pltpu.VMEM((rows, 1), F32), pltpu.VMEM((rows, dh), F32),
                        pltpu.VMEM((kv_block[-1], kv_new.shape[2]), F32)])
    return pl.pallas_call(
        functools.partial(_sb_dec_kernel, pps=pps, n_new=n_new),
        grid_spec=grid_spec,
        out_shape=jax.ShapeDtypeStruct((db, rows, dh), F32),
        compiler_params=_cparams(2),
    )(page_table.reshape(-1), q_rows, kv_new, *([cache_kv] * pps))


def _layer_norm(v, g, b):
    mu = jnp.mean(v, axis=-1, keepdims=True)
    var = jnp.mean(jnp.square(v - mu), axis=-1, keepdims=True)
    return (v - mu) * lax.rsqrt(var + LN_EPS) * g + b


def _merge_kernel(x_ref, mod_ref, ba_ref, bb_ref, bc_ref, bd_ref, wg_ref, wb_ref, wo_ref, g_ref, b_ref,
                  wr_ref, br_ref, cnt0_ref, x1_ref, h2_ref, ti_ref, tp_ref, rk_ref, cnt_ref, *, alpha):
    @pl.when(pl.program_id(0) == 0)
    def _():
        cnt_ref[...] = cnt0_ref[...]

    x = x_ref[...]
    tm, d = x.shape
    h = (x * (1.0 + mod_ref[1]) + mod_ref[0]).astype(BF16)
    y = None
    for n, br in enumerate((ba_ref, bb_ref, bc_ref, bd_ref)):
        gate = jax.nn.sigmoid(_dot(h, wg_ref[:, n * d:(n + 1) * d]))
        term = gate * _dot(br[...], wb_ref[n])
        y = term if y is None else y + term
    y = _dot(y.astype(BF16), wo_ref[...])
    x1 = _layer_norm(alpha * x + (1.0 + mod_ref[2]) * y, g_ref[...], b_ref[...])
    x1_ref[...] = x1
    h2 = x1 * (1.0 + mod_ref[4]) + mod_ref[3]
    h2_ref[...] = h2
    logits = _dot(h2.astype(BF16), wr_ref[...]) + br_ref[...]
    ne = logits.shape[1]
    lane = lax.broadcasted_iota(jnp.int32, logits.shape, 1).astype(F32)
    kcol = lax.broadcasted_iota(jnp.int32, (tm, TOP_K), 1)
    tv = jnp.zeros((tm, TOP_K), F32)
    ti = jnp.zeros((tm, TOP_K), F32)
    picks = []
    for k in range(TOP_K):
        m = jnp.max(logits, axis=-1, keepdims=True)
        idx = jnp.min(jnp.where(logits == m, lane, float(ne)), axis=-1, keepdims=True)
        pick = lane == idx
        picks.append(pick)
        tv = jnp.where(kcol == k, m, tv)
        ti = jnp.where(kcol == k, idx, ti)
        logits = jnp.where(pick, -jnp.inf, logits)
    e = jnp.exp(tv - tv[:, 0:1])
    ti_ref[...] = ti.astype(jnp.int32)
    tp_ref[...] = e / jnp.sum(e, axis=-1, keepdims=True)
    chosen = None
    for pick in picks:
        one = jnp.where(pick, 1.0, 0.0)
        chosen = one if chosen is None else chosen + one
    r = lax.broadcasted_iota(jnp.int32, (tm, tm), 0)
    c = lax.broadcasted_iota(jnp.int32, (tm, tm), 1)
    earlier = jnp.where(c < r, 1.0, 0.0).astype(BF16)
    base = cnt_ref[...] + _dot(earlier, chosen.astype(BF16))
    rk = jnp.zeros((tm, TOP_K), F32)
    for k, pick in enumerate(picks):
        rk = jnp.where(kcol == k, jnp.sum(jnp.where(pick, base, 0.0), axis=-1, keepdims=True), rk)
    rk_ref[...] = rk.astype(jnp.int32)
    cnt_ref[...] = cnt_ref[...] + jnp.sum(chosen, axis=0, keepdims=True)


def _merge(x2d, mod, branches, w_gate, w_branch, w_out, ln_g, ln_b, w_router, b_router, counts, tm, rows_per_seq, alpha):
    n, d = x2d.shape
    full = lambda a: pl.BlockSpec(a.shape, lambda i: (0,) * a.ndim)
    row = lambda w: pl.BlockSpec((tm, w), lambda i: (i, 0))
    return pl.pallas_call(
        functools.partial(_merge_kernel, alpha=alpha),
        grid=(n // tm,),
        in_specs=[row(d), _mod_spec(mod, tm, rows_per_seq)] + [row(b.shape[1]) for b in branches]
        + [full(w_gate), full(w_branch), full(w_out), full(ln_g), full(ln_b), full(w_router), full(b_router),
           full(counts)],
        out_specs=[row(d), row(d), row(TOP_K), row(TOP_K), row(TOP_K), full(counts)],
        out_shape=[jax.ShapeDtypeStruct((n, d), F32), jax.ShapeDtypeStruct((n, d), F32),
                   jax.ShapeDtypeStruct((n, TOP_K), jnp.int32), jax.ShapeDtypeStruct((n, TOP_K), F32),
                   jax.ShapeDtypeStruct((n, TOP_K), jnp.int32), jax.ShapeDtypeStruct(counts.shape, F32)],
        compiler_params=_cparams(1),
    )(x2d, mod, *branches, w_gate, w_branch, w_out, ln_g, ln_b, w_router, b_router, counts)


def _expert_kernel(be_ref, nb_ref, x_ref, w1_ref, b1g_ref, b1l_ref, w2_ref, b2_ref, o_ref,
                   wt_ref, wg_ref, wl_ref, w2b_ref):
    i = pl.program_id(0)
    f = wg_ref.shape[0]

    @pl.when((i == 0) | (be_ref[i] != be_ref[jnp.maximum(i - 1, 0)]))
    def _():
        for c in range(wt_ref.shape[0]):
            cols = slice(c * LANES, (c + 1) * LANES)
            wt_ref[c] = w1_ref[cols, :].T
            wg_ref[:, cols] = wt_ref[c, pl.ds(0, f, stride=2), :].astype(BF16)
            wl_ref[:, cols] = wt_ref[c, pl.ds(1, f, stride=2), :].astype(BF16)
        w2b_ref[...] = w2_ref[...].astype(BF16)

    @pl.when(i < nb_ref[0])
    def _():
        x = x_ref[...].astype(BF16)
        glu = jnp.minimum(_nt(x, wg_ref[...]) + b1g_ref[...], SWIGLU_LIMIT)
        lin = jnp.clip(_nt(x, wl_ref[...]) + b1l_ref[...], -SWIGLU_LIMIT, SWIGLU_LIMIT)
        act = glu * jax.nn.sigmoid(SWIGLU_ALPHA * glu) * (lin + 1.0)
        o_ref[...] = _dot(act.astype(BF16), w2b_ref[...]) + b2_ref[...]


def _experts(layer, block_exp, n_used, xs, w1, b1g, b1l, w2, b2):
    n_slots, d = xs.shape
    f2 = w1.shape[-1]
    f = f2 // 2
    nb = n_slots // MOE_ROWS
    blk = lambda i, be, nu: (jnp.minimum(i, nu[0] - 1), 0)
    expert = lambda i, be, nu: be[jnp.minimum(i, nu[0] - 1)]
    wspec = lambda s: pl.BlockSpec((None, None) + s, lambda i, be, nu: (layer, expert(i, be, nu), 0, 0))
    bspec = lambda w: pl.BlockSpec((None, 1, w), lambda i, be, nu: (expert(i, be, nu), 0, 0))
    grid_spec = pltpu.PrefetchScalarGridSpec(
        num_scalar_prefetch=2,
        grid=(nb,),
        in_specs=[pl.BlockSpec((MOE_ROWS, d), blk), wspec((d, f2)), bspec(f), bspec(f), wspec((f, d)), bspec(d)],
        out_specs=pl.BlockSpec((MOE_ROWS, d), blk),
        scratch_shapes=[pltpu.VMEM((d // LANES, f2, LANES), F32), pltpu.VMEM((f, d), BF16), pltpu.VMEM((f, d), BF16),
                        pltpu.VMEM((f, d), BF16)])
    return pl.pallas_call(
        _expert_kernel,
        grid_spec=grid_spec,
        out_shape=jax.ShapeDtypeStruct((n_slots, d), F32),
        compiler_params=_cparams(1),
    )(block_exp, n_used, xs, w1, b1g, b1l, w2, b2)


def _final_kernel(x_ref, mod_ref, *rest, alpha):
    y_refs, (p_ref, g_ref, b_ref, o_ref) = rest[:TOP_K], rest[TOP_K:]
    p = p_ref[...]
    f = None
    for k in range(TOP_K):
        term = p[:, k:k + 1] * y_refs[k][...]
        f = term if f is None else f + term
    o_ref[...] = _layer_norm(alpha * x_ref[...] + (1.0 + mod_ref[5]) * f, g_ref[...], b_ref[...])


def _final(x1, mod, ys, probs, ln_g, ln_b, tm, rows_per_seq, alpha):
    n, d = x1.shape
    full = lambda a: pl.BlockSpec(a.shape, lambda i: (0,) * a.ndim)
    return pl.pallas_call(
        functools.partial(_final_kernel, alpha=alpha),
        grid=(n // tm,),
        in_specs=[pl.BlockSpec((tm, d), lambda i: (i, 0)), _mod_spec(mod, tm, rows_per_seq)]
        + [pl.BlockSpec((tm, d), lambda i: (i, 0))] * TOP_K
        + [pl.BlockSpec((tm, TOP_K), lambda i: (i, 0)), full(ln_g), full(ln_b)],
        out_specs=pl.BlockSpec((tm, d), lambda i: (i, 0)),
        out_shape=jax.ShapeDtypeStruct((n, d), F32),
        compiler_params=_cparams(1),
    )(x1, mod, *ys, probs, ln_g, ln_b)


def _rope_tables(pos, rope):
    half = rope // 2
    inv = ROPE_BASE ** (-jnp.arange(half, dtype=F32) / half)
    ang = pos.astype(F32)[:, None] * inv[None, :]
    cos, sin = jnp.cos(ang), jnp.sin(ang)
    reps = LANES // rope
    return (jnp.tile(jnp.concatenate([cos, cos], -1), (1, reps)),
            jnp.tile(jnp.concatenate([-sin, sin], -1), (1, reps)))


def _swap_halves(w):
    half = w.shape[-1] // 2
    return jnp.concatenate([w[..., half:], w[..., :half]], axis=-1)


def _layer_weights(l, p, dims):
    d, dp, hd, q_lora, kv_lora, rope = dims
    w_a, w_gate = _win_layout(p['w_in'][l], dims)
    eye = jnp.eye(N_HEADS, dtype=F32)
    w_uq = p['w_mla_uq'][l]
    nope = w_uq.shape[-1] - rope
    pad = lambda w: jnp.pad(w, ((0, 0), (0, 0), (0, LANES - rope))).reshape(q_lora, N_HEADS * LANES)
    w_rope = w_uq[:, :, nope:]
    lw = {
        'w_a': w_a,
        'w_gate': w_gate,
        'w_pool_bd': jnp.einsum('gcd,gh->gchd', p['w_pool_mix'][l], jnp.eye(len(POOL_WINDOWS), dtype=F32)
                                ).reshape(dp, dp).astype(BF16),
        'pool_scale': p['pool_scale'][l].reshape(1, dp),
        'fox_bias': jnp.zeros((1, LANES), F32).at[0, 2 * rope:2 * rope + N_HEADS].set(p['b_fox_forget'][l]),
        'g_q': p['mla_q_norm'][l].reshape(1, q_lora),
        'g_kv': p['mla_kv_norm'][l].reshape(1, kv_lora),
        'w_nope': w_uq[:, :, :nope].reshape(q_lora, N_HEADS * nope).astype(BF16),
        'w_rope': pad(w_rope).astype(BF16),
        'w_rope_sw': pad(_swap_halves(w_rope)).astype(BF16),
        'w_uk_bd': jnp.einsum('chn,hg->hngc', p['w_mla_uk'][l], eye).reshape(N_HEADS * nope, N_HEADS * kv_lora).astype(BF16),
        'w_uv_bd': jnp.einsum('chv,hg->hcgv', p['w_mla_uv'][l], eye).reshape(N_HEADS * kv_lora, -1).astype(BF16),
        'w_uv_all': p['w_mla_uv'][l].reshape(kv_lora, -1).astype(BF16),
        'w_branch': p['w_branch'][l].astype(BF16),
        'w_out': p['w_out'][l].astype(BF16),
        'ln1_g': p['ln1_g'][l].reshape(1, d), 'ln1_b': p['ln1_b'][l].reshape(1, d),
        'ln2_g': p['ln2_g'][l].reshape(1, d), 'ln2_b': p['ln2_b'][l].reshape(1, d),
        'b_router': p['b_router'][l].reshape(1, -1),
        'b1g': p['b_exp1'][l][:, None, 0::2], 'b1l': p['b_exp1'][l][:, None, 1::2],
        'b2': p['b_exp2'][l][:, None, :],
        'w_router': p['w_router'][l].astype(BF16),
    }
    lw['widths'] = (dp, 3 * hd, q_lora + kv_lora, 3 * hd, LANES)
    lw['nope'] = nope
    return lw


def _routing(top_i, rank, counts):
    n = top_i.shape[0]
    nk = n * TOP_K
    n_experts = counts.shape[-1]
    sizes = counts.reshape(n_experts).astype(jnp.int32)
    padded = (sizes + MOE_ROWS - 1) // MOE_ROWS * MOE_ROWS
    ends = jnp.cumsum(padded)
    starts = ends - padded
    first = jnp.cumsum(sizes) - sizes
    dest = starts[top_i] + rank
    n_blocks = -(-(nk + n_experts * (MOE_ROWS - 1)) // MOE_ROWS)
    blk_start = jnp.arange(n_blocks, dtype=jnp.int32) * MOE_ROWS
    block_exp = jnp.minimum(jnp.sum(ends[None, :] <= blk_start[:, None], axis=1), n_experts - 1).astype(jnp.int32)
    order = jnp.argsort(top_i.reshape(nk))
    slot = jnp.arange(n_blocks * MOE_ROWS, dtype=jnp.int32)
    e_slot = jnp.repeat(block_exp, MOE_ROWS)
    r = slot - starts[e_slot]
    pair = order[jnp.clip(first[e_slot] + r, 0, nk - 1)]
    slot_tok = jnp.where(r < sizes[e_slot], pair // TOP_K, 0).astype(jnp.int32)
    n_used = (ends[-1] // MOE_ROWS).astype(jnp.int32).reshape(1)
    return slot_tok, dest, block_exp, n_used


def _head_rows(q, scale):
    db, dt, hd = q.shape
    dh = hd // N_HEADS
    q4 = jnp.transpose((q * scale).reshape(db, dt, N_HEADS, dh), (0, 2, 1, 3))
    q4 = jnp.pad(q4, ((0, 0), (0, 0), (0, DEC_ROWS - dt), (0, 0)))
    return q4.reshape(db, N_HEADS * DEC_ROWS, dh).astype(BF16)


def _head_out(o, dt):
    db, _, dv = o.shape
    o4 = jnp.transpose(o.reshape(db, N_HEADS, DEC_ROWS, dv)[:, :, :dt], (0, 2, 1, 3))
    return o4.reshape(db * dt, N_HEADS * dv).astype(BF16)


def _diag_heads(o, dt):
    db, _, hv = o.shape
    dv = hv // N_HEADS
    o5 = o.reshape(db, N_HEADS, dt, N_HEADS, dv)
    d = jnp.stack([o5[:, h, :, h, :] for h in range(N_HEADS)], axis=2)
    return d.reshape(db * dt, hv).astype(BF16)


def kernel(x_prompt, x_sample, c_prompt, c_sample, cache_fox_kv, cache_fox_logf, cache_mla, cache_sb_kv, state_pool, page_table, w_ada, b_ada, w_in, b_fox_forget, w_pool_mix, pool_scale, mla_q_norm, w_mla_uq, mla_kv_norm, w_mla_uk, w_mla_uv, w_branch, w_out, ln1_g, ln1_b, w_router, b_router, w_exp1, b_exp1, w_exp2, b_exp2, ln2_g, ln2_b):
    p = dict(w_in=w_in, b_fox_forget=b_fox_forget, w_pool_mix=w_pool_mix, pool_scale=pool_scale,
             mla_q_norm=mla_q_norm, w_mla_uq=w_mla_uq, mla_kv_norm=mla_kv_norm, w_mla_uk=w_mla_uk,
             w_mla_uv=w_mla_uv, w_branch=w_branch, w_out=w_out, ln1_g=ln1_g, ln1_b=ln1_b, w_router=w_router,
             b_router=b_router, w_exp1=w_exp1, b_exp1=b_exp1, w_exp2=w_exp2, b_exp2=b_exp2, ln2_g=ln2_g, ln2_b=ln2_b)
    b, t, d = x_prompt.shape
    db, dt, _ = x_sample.shape
    depth = w_ada.shape[0]
    n_pages = page_table.shape[1]
    page = cache_fox_kv.shape[2]
    past = n_pages * page
    dp = state_pool.shape[-1]
    hd = cache_fox_kv.shape[-1] * cache_fox_kv.shape[-2]
    dh = hd // N_HEADS
    q_lora = mla_q_norm.shape[1]
    kv_lora = mla_kv_norm.shape[1]
    rope = cache_mla.shape[-1] - kv_lora
    n_experts = w_router.shape[-1]
    alpha = (2 * depth) ** 0.25
    dims = (d, dp, hd, q_lora, kv_lora, rope)
    np_rows, ns_rows = b * t, db * dt
    tm = min(ROW_TILE, t)
    tile = min(ATT_TILE, t)
    sb_tile = min(SB_TILE, t)
    nq = t // tile

    mods = _ada(jnp.concatenate([c_prompt, c_sample], axis=0), w_ada, b_ada)
    cos_p, sin_p = _rope_tables(jnp.arange(t), rope)
    cos_s, sin_s = (jnp.tile(a, (db, 1)) for a in _rope_tables(past + jnp.arange(dt), rope))
    fox_kv_t = jnp.transpose(cache_fox_kv, (0, 1, 3, 4, 5, 2))
    sb_kv_t = jnp.transpose(cache_sb_kv, (0, 1, 3, 4, 5, 2))
    mla_t = jnp.transpose(cache_mla, (0, 1, 3, 2))
    logf_t = jnp.transpose(cache_fox_logf, (0, 1, 3, 2))
    sfx = _fox_sfx(logf_t.reshape(-1, page)).reshape(logf_t.shape[:3] + (2 * page,))
    zero_counts = jnp.zeros((1, n_experts), F32)

    xp = x_prompt.reshape(np_rows, d)
    xs = x_sample.reshape(ns_rows, d)
    outs = {k: [] for k in ('fox_p', 'fox_s', 'lf_p', 'lf_s', 'mla_p', 'mla_s', 'sb_p', 'sb_s', 'pool_p', 'pool_s')}
    for l in range(depth):
        lw = _layer_weights(l, p, dims)
        mod_p = jnp.transpose(mods[l, :, :b], (1, 0, 2))[:, :, None, :]
        mod_s = jnp.repeat(mods[l, :, b:], dt, axis=1)[None]
        mla_scale = (lw['nope'] + rope) ** -0.5

        u_pool, u_fox, u_mla, u_sb, u_small = _inproj(xp, mod_p, lw['w_a'], lw['widths'], tm, t)
        br_a, pool_new = _pool(u_pool.reshape(b, t, dp), jnp.zeros((b, POOL_BUF, dp), F32),
                               lw['w_pool_bd'], lw['pool_scale'], 0)
        lf, cum = _fox_prep(u_small.reshape(b, t, LANES), lw['fox_bias'], 2 * rope)
        c_s = jnp.transpose(cum, (0, 2, 1)).reshape(b, N_HEADS, nq, 1, tile)
        br_b = _fox_attn(u_fox.reshape(b, t, 3 * hd), cum, c_s, tile)
        q_cat, lat_pad, lat_new = _mla_prep(u_mla, u_small, cos_p, sin_p, lw['g_q'], lw['g_kv'], lw['w_nope'],
                                            lw['w_rope'], lw['w_rope_sw'], lw['w_uk_bd'], tm, rope)
        br_c = _mla_attn(q_cat.reshape(b, t, -1), lat_pad.reshape(b, t, -1), lw['w_uv_bd'], tile, kv_lora, mla_scale)
        br_d = _sb_attn(u_sb.reshape(b, t, 3 * hd), sb_tile)
        branches = [br_a.reshape(np_rows, dp), br_b.reshape(np_rows, hd), br_c.reshape(np_rows, hd),
                    br_d.reshape(np_rows, hd)]
        x1_p, h2_p, ti_p, tp_p, rk_p, counts = _merge(
            xp, mod_p, branches, lw['w_gate'], lw['w_branch'], lw['w_out'], lw['ln1_g'], lw['ln1_b'],
            lw['w_router'], lw['b_router'], zero_counts, tm, t, alpha)
        outs['fox_p'].append(u_fox[:, hd:].reshape(b, t, 2, N_HEADS, dh))
        outs['lf_p'].append(lf)
        outs['mla_p'].append(lat_new.reshape(b, t, kv_lora + rope))
        outs['sb_p'].append(u_sb[:, hd:].reshape(b, t, 2, N_HEADS, dh))
        outs['pool_p'].append(pool_new)

        u_pool, u_fox, u_mla, u_sb, u_small = _inproj(xs, mod_s, lw['w_a'], lw['widths'], ns_rows, dt)
        br_a, pool_new = _pool(u_pool.reshape(db, dt, dp), state_pool[l], lw['w_pool_bd'], lw['pool_scale'], past)
        lf, cum = _fox_prep(u_small.reshape(db, dt, LANES), lw['fox_bias'], 2 * rope)
        u_fox3 = u_fox.reshape(db, dt, 3 * hd)
        g_new = jnp.broadcast_to(-jnp.transpose(cum, (0, 2, 1))[:, :, None, :], (db, N_HEADS, DEC_ROWS, dt))
        g_new = jnp.pad(g_new.reshape(db, N_HEADS * DEC_ROWS, dt), ((0, 0), (0, 0), (0, LANES - dt)))
        o_fox = _fox_dec(l, _head_rows(u_fox3[:, :, :hd], dh ** -0.5), u_fox3[:, :, hd:], g_new,
                         fox_kv_t, sfx, page_table, dt)
        q_cat, lat_pad, lat_new = _mla_prep(u_mla, u_small, cos_s, sin_s, lw['g_q'], lw['g_kv'], lw['w_nope'],
                                            lw['w_rope'], lw['w_rope_sw'], lw['w_uk_bd'], ns_rows, rope)
        wq = kv_lora + LANES
        q_rows = q_cat.reshape(db, dt, N_HEADS, wq)[..., :kv_lora + rope]
        q_rows = jnp.transpose(q_rows, (0, 2, 1, 3)).reshape(db, N_HEADS * dt, kv_lora + rope)
        o_mla = _mla_dec(l, q_rows, lat_new.reshape(db, dt, -1), lw['w_uv_all'], mla_t, page_table, dt, kv_lora,
                         mla_scale)
        u_sb3 = u_sb.reshape(db, dt, 3 * hd)
        o_sb = _sb_dec(l, _head_rows(u_sb3[:, :, :hd], dh ** -0.5), u_sb3[:, :, hd:],
                       sb_kv_t, page_table, dt)
        branches = [br_a.reshape(ns_rows, dp), _head_out(o_fox, dt), _diag_heads(o_mla, dt), _head_out(o_sb, dt)]
        x1_s, h2_s, ti_s, tp_s, rk_s, counts = _merge(
            xs, mod_s, branches, lw['w_gate'], lw['w_branch'], lw['w_out'], lw['ln1_g'], lw['ln1_b'],
            lw['w_router'], lw['b_router'], counts, ns_rows, dt, alpha)
        outs['fox_s'].append(u_fox3[:, :, hd:].reshape(db, dt, 2, N_HEADS, dh))
        outs['lf_s'].append(lf)
        outs['mla_s'].append(lat_new.reshape(db, dt, kv_lora + rope))
        outs['sb_s'].append(u_sb3[:, :, hd:].reshape(db, dt, 2, N_HEADS, dh))
        outs['pool_s'].append(pool_new)

        h2 = jnp.concatenate([h2_p, h2_s], axis=0)
        slot_tok, dest, block_exp, n_used = _routing(jnp.concatenate([ti_p, ti_s], axis=0),
                                                     jnp.concatenate([rk_p, rk_s], axis=0), counts)
        y_slots = _experts(l, block_exp, n_used, h2[slot_tok], w_exp1, lw['b1g'], lw['b1l'], w_exp2, lw['b2'])
        y_p = [y_slots[dest[:np_rows, k]] for k in range(TOP_K)]
        y_s = [y_slots[dest[np_rows:, k]] for k in range(TOP_K)]
        xp = _final(x1_p, mod_p, y_p, tp_p, lw['ln2_g'], lw['ln2_b'], tm, t, alpha)
        xs = _final(x1_s, mod_s, y_s, tp_s, lw['ln2_g'], lw['ln2_b'], ns_rows, dt, alpha)

    st = lambda k: jnp.stack(outs[k])
    return (xp.reshape(b, t, d), xs.reshape(db, dt, d), st('fox_p'), st('fox_s'), st('lf_p'), st('lf_s'),
            st('mla_p'), st('mla_s'), st('sb_p'), st('sb_s'), st('pool_p'), st('pool_s'))
```

```python
import functools

import jax
import jax.numpy as jnp
from jax import lax
from jax.experimental import pallas as pl
from jax.experimental.pallas import tpu as pltpu

F32 = jnp.float32
BF16 = jnp.bfloat16

POOL_WINDOWS = (2, 4, 8, 16)
POOL_BUF = max(POOL_WINDOWS) - 1
N_HEADS = 4
N_BRANCH = 4
N_MOD = 6
TOP_K = 4
ROPE_BASE = 10000.0
SWIGLU_ALPHA = 1.702
SWIGLU_LIMIT = 7.0
LN_EPS = 1e-5
RMS_EPS = 1e-6

LANES = 128
SUBLANES = 8
VMEM_LIMIT = 56 * 1024 * 1024

NEG = -1e30

ROW_TILE = 512
ATT_TILE = 512
SB_TILE = 256
PAGES_PER_STEP = 32
MOE_ROWS = 256
DEC_ROWS = 16


def _cparams(n_axes):
    return pltpu.CompilerParams(dimension_semantics=("arbitrary",) * n_axes, vmem_limit_bytes=VMEM_LIMIT)


def _nt(a, b):
    return lax.dot_general(a, b, (((1,), (1,)), ((), ())), preferred_element_type=F32)


def _dot(a, b):
    return jnp.dot(a, b, preferred_element_type=F32)


def _split3(x):
    hi = x.astype(BF16)
    r = x - hi.astype(F32)
    mid = r.astype(BF16)
    lo = (r - mid.astype(F32)).astype(BF16)
    return hi, mid, lo


def _dot_f32_lhs(x, m01):
    hi, mid, lo = _split3(x)
    return _dot(hi, m01) + _dot(mid, m01) + _dot(lo, m01)


def _dot_f32_lhs2(x, m01):
    hi = x.astype(BF16)
    lo = (x - hi.astype(F32)).astype(BF16)
    return _dot(hi, m01) + _dot(lo, m01)


def _dot_f32_rhs(m01, x):
    hi, mid, lo = _split3(x)
    return _dot(m01, hi) + _dot(m01, mid) + _dot(m01, lo)


def _log_sigmoid(x):
    return jnp.minimum(x, 0.0) - jnp.log(1.0 + jnp.exp(-jnp.abs(x)))


def _mod_spec(mod, tm, rows_per_seq):
    d = mod.shape[-1]
    if mod.shape[2] == 1:
        return pl.BlockSpec((None, N_MOD, 1, d), lambda i: ((i * tm) // rows_per_seq, 0, 0, 0))
    return pl.BlockSpec((None, N_MOD, tm, d), lambda i: (0, 0, i, 0))


def _ada_kernel(c_ref, w_ref, b_ref, o_ref):
    c = c_ref[...]
    s = c * jax.nn.sigmoid(c)
    o_ref[...] = _dot(s.astype(BF16), w_ref[...].astype(BF16)) + b_ref[...]


def _ada(c_all, w_ada, b_ada):
    depth, d, _ = w_ada.shape
    r = c_all.shape[0]
    b4 = b_ada.reshape(depth, N_MOD, 1, d)
    return pl.pallas_call(
        _ada_kernel,
        grid=(depth, N_MOD),
        in_specs=[pl.BlockSpec((r, d), lambda l, k: (0, 0)),
                  pl.BlockSpec((None, d, d), lambda l, k: (l, 0, k)),
                  pl.BlockSpec((None, None, 1, d), lambda l, k: (l, k, 0, 0))],
        out_specs=pl.BlockSpec((None, None, r, d), lambda l, k: (l, k, 0, 0)),
        out_shape=jax.ShapeDtypeStruct((depth, N_MOD, r, d), F32),
        compiler_params=_cparams(2),
    )(c_all, w_ada, b4)


def _win_layout_kernel(w_ref, wa_ref, wg_ref, *, src, dst, rope_src, rope_dst, gate_src):
    wa_ref[...] = jnp.zeros(wa_ref.shape, wa_ref.dtype)
    for (s0, s1), d0 in zip(src, dst):
        wa_ref[:, d0:d0 + (s1 - s0)] = w_ref[:, s0:s1].astype(wa_ref.dtype)
    half = (rope_src[1] - rope_src[0]) // 2
    wa_ref[:, rope_dst:rope_dst + half] = w_ref[:, rope_src[0] + half:rope_src[1]].astype(wa_ref.dtype)
    wa_ref[:, rope_dst + half:rope_dst + 2 * half] = w_ref[:, rope_src[0]:rope_src[0] + half].astype(wa_ref.dtype)
    wg_ref[...] = w_ref[:, gate_src:gate_src + wg_ref.shape[1]].astype(wg_ref.dtype)


def _win_layout(w_in_l, dims):
    d, dp, hd, q_lora, kv_lora, rope = dims
    widths = (dp, 3 * hd, N_HEADS, q_lora, kv_lora, rope, 3 * hd, N_BRANCH * d)
    offs = [0]
    for w in widths:
        offs.append(offs[-1] + w)
    pool, fox, f, cq, ckv, kr, sb, gate = [(offs[i], offs[i + 1]) for i in range(len(widths))]
    small = dp + 3 * hd + q_lora + kv_lora + 3 * hd
    src = (pool, fox, cq, ckv, sb, kr, f)
    dst = (0, dp, dp + 3 * hd, dp + 3 * hd + q_lora, dp + 3 * hd + q_lora + kv_lora, small, small + 2 * rope)
    rows = LANES
    return pl.pallas_call(
        functools.partial(_win_layout_kernel, src=src, dst=dst, rope_src=kr, rope_dst=small + rope, gate_src=gate[0]),
        grid=(d // rows,),
        in_specs=[pl.BlockSpec((rows, w_in_l.shape[1]), lambda i: (i, 0))],
        out_specs=[pl.BlockSpec((rows, small + LANES), lambda i: (i, 0)),
                   pl.BlockSpec((rows, N_BRANCH * d), lambda i: (i, 0))],
        out_shape=[jax.ShapeDtypeStruct((d, small + LANES), BF16), jax.ShapeDtypeStruct((d, N_BRANCH * d), BF16)],
        compiler_params=_cparams(1),
    )(w_in_l)


def _inproj_kernel(x_ref, mod_ref, w_ref, *out_refs, widths):
    h = x_ref[...] * (1.0 + mod_ref[1]) + mod_ref[0]
    u = _dot(h.astype(BF16), w_ref[...])
    off = 0
    for o_ref, w in zip(out_refs, widths):
        o_ref[...] = u[:, off:off + w]
        off += w


def _inproj(x2d, mod, w_a, widths, tm, rows_per_seq):
    n, d = x2d.shape
    return pl.pallas_call(
        functools.partial(_inproj_kernel, widths=widths),
        grid=(n // tm,),
        in_specs=[pl.BlockSpec((tm, d), lambda i: (i, 0)),
                  _mod_spec(mod, tm, rows_per_seq),
                  pl.BlockSpec(w_a.shape, lambda i: (0, 0))],
        out_specs=[pl.BlockSpec((tm, w), lambda i: (i, 0)) for w in widths],
        out_shape=[jax.ShapeDtypeStruct((n, w), F32) for w in widths],
        compiler_params=_cparams(1),
    )(x2d, mod, w_a)


def _pool_kernel(u_ref, pre_ref, w_ref, sc_ref, o_ref, new_ref, z_ref, *, t, pos0):
    total, dp = z_ref.shape
    base = POOL_BUF + 1
    z_ref[0:1, :] = jnp.zeros((1, dp), F32)
    z_ref[1:base, :] = pre_ref[...]
    z_ref[base:base + t, :] = u_ref[...]
    if total > base + t:
        z_ref[base + t:total, :] = jnp.zeros((total - base - t, dp), F32)
    z = z_ref[...]
    sums = []
    s = z
    for step in (1, 2, 4, 8):
        s = s + pltpu.roll(s, step, 0)
        sums.append(s)
    lane = lax.broadcasted_iota(jnp.int32, (t, dp), 1)
    group = dp // len(POOL_WINDOWS)
    win = sums[-1][base:base + t]
    wlen = jnp.full((t, dp), float(POOL_WINDOWS[-1]), F32)
    for g in range(len(POOL_WINDOWS) - 2, -1, -1):
        sel = lane < (g + 1) * group
        win = jnp.where(sel, sums[g][base:base + t], win)
        wlen = jnp.where(sel, float(POOL_WINDOWS[g]), wlen)
    pos = (pos0 + lax.broadcasted_iota(jnp.int32, (t, dp), 0)).astype(F32)
    cnt = jnp.minimum(wlen, pos + 1.0)
    pooled = win / cnt - u_ref[...]
    o_ref[...] = (_dot(pooled.astype(BF16), w_ref[...]) * sc_ref[...]).astype(o_ref.dtype)
    new_ref[...] = z_ref[t + 1:t + 1 + POOL_BUF, :]


def _pool(u_pool, prefix, w_bd, scale, pos0):
    b, t, dp = u_pool.shape
    total = -(-(POOL_BUF + 1 + t) // SUBLANES) * SUBLANES
    return pl.pallas_call(
        functools.partial(_pool_kernel, t=t, pos0=pos0),
        grid=(b,),
        in_specs=[pl.BlockSpec((None, t, dp), lambda i: (i, 0, 0)),
                  pl.BlockSpec((None, POOL_BUF, dp), lambda i: (i, 0, 0)),
                  pl.BlockSpec((dp, dp), lambda i: (0, 0)),
                  pl.BlockSpec((1, dp), lambda i: (0, 0))],
        out_specs=[pl.BlockSpec((None, t, dp), lambda i: (i, 0, 0)),
                   pl.BlockSpec((None, POOL_BUF, dp), lambda i: (i, 0, 0))],
        out_shape=[jax.ShapeDtypeStruct((b, t, dp), BF16),
                   jax.ShapeDtypeStruct((b, POOL_BUF, dp), F32)],
        scratch_shapes=[pltpu.VMEM((total, dp), F32)],
        compiler_params=_cparams(1),
    )(u_pool, prefix, w_bd, scale)


def _fox_prep_kernel(s_ref, b_ref, lf_ref, c_ref, *, t, col):
    x = s_ref[...] + b_ref[...]
    lf = _log_sigmoid(x)
    lf_ref[...] = lf[:, col:col + N_HEADS]
    if t % LANES == 0:
        r = lax.broadcasted_iota(jnp.int32, (LANES, LANES), 0)
        c = lax.broadcasted_iota(jnp.int32, (LANES, LANES), 1)
        tri = jnp.where(r >= c, 1.0, 0.0).astype(BF16)
        carry = jnp.zeros((1, LANES), F32)
        for ch in range(t // LANES):
            cs = _dot_f32_rhs(tri, lf[ch * LANES:(ch + 1) * LANES]) + carry
            c_ref[ch * LANES:(ch + 1) * LANES, :] = cs[:, col:col + N_HEADS]
            carry = cs[LANES - 1:LANES, :]
    else:
        acc = jnp.zeros((1, LANES), F32)
        for i in range(t):
            acc = acc + lf[i:i + 1]
            c_ref[i:i + 1, :] = acc[:, col:col + N_HEADS]


def _fox_prep(small, bias_row, col):
    b, t, w = small.shape
    return pl.pallas_call(
        functools.partial(_fox_prep_kernel, t=t, col=col),
        grid=(b,),
        in_specs=[pl.BlockSpec((None, t, w), lambda i: (i, 0, 0)),
                  pl.BlockSpec((1, w), lambda i: (0, 0))],
        out_specs=[pl.BlockSpec((None, t, N_HEADS), lambda i: (i, 0, 0))] * 2,
        out_shape=[jax.ShapeDtypeStruct((b, t, N_HEADS), F32)] * 2,
        compiler_params=_cparams(1),
    )(small, bias_row)


def _softmax_step(carry, s, v):
    m, l, acc = carry
    m_new = jnp.maximum(m, jnp.max(s, axis=-1, keepdims=True))
    p = jnp.exp(s - m_new)
    alpha = jnp.exp(m - m_new)
    l = alpha * l + jnp.sum(p, axis=-1, keepdims=True)
    acc = alpha * acc + _dot(p.astype(BF16), v)
    return m_new, l, acc


def _causal_mask(tq, tk, strict):
    r = lax.broadcasted_iota(jnp.int32, (tq, tk), 0)
    c = lax.broadcasted_iota(jnp.int32, (tq, tk), 1)
    return (c < r) if strict else (c <= r)


def _fox_attn_kernel(q_ref, k_ref, v_ref, ct_ref, cs_ref, o_ref, *, tile, dh):
    i = pl.program_id(1)
    scale = dh ** -0.5
    mask = _causal_mask(tile, tile, False)
    heads = [slice(h * dh, (h + 1) * dh) for h in range(N_HEADS)]
    qs = [(q_ref[:, sl] * scale).astype(BF16) for sl in heads]
    cts = [ct_ref[:, h:h + 1] for h in range(N_HEADS)]

    def step(j, carry, masked):
        rows = pl.ds(pl.multiple_of(j * tile, tile), tile)
        out = []
        for h, sl in enumerate(heads):
            s = _nt(qs[h], k_ref[rows, sl].astype(BF16)) + (cts[h] - cs_ref[h, j])
            if masked:
                s = jnp.where(mask, s, NEG)
            out.append(_softmax_step(carry[h], s, v_ref[rows, sl].astype(BF16)))
        return tuple(out)

    init = tuple((jnp.full((tile, 1), NEG, F32), jnp.zeros((tile, 1), F32), jnp.zeros((tile, dh), F32))
                 for _ in heads)
    carry = lax.fori_loop(0, i, lambda j, c: step(j, c, False), init)
    carry = step(i, carry, True)
    for sl, (m, l, acc) in zip(heads, carry):
        o_ref[:, sl] = (acc / l).astype(o_ref.dtype)


def _fox_attn(u_fox, c_t, c_s, tile):
    b, t, w = u_fox.shape
    hd = w // 3
    dh = hd // N_HEADS
    nq = t // tile
    return pl.pallas_call(
        functools.partial(_fox_attn_kernel, tile=tile, dh=dh),
        grid=(b, nq),
        in_specs=[pl.BlockSpec((None, tile, hd), lambda bi, i: (bi, i, 0)),
                  pl.BlockSpec((None, t, hd), lambda bi, i: (bi, 0, 1)),
                  pl.BlockSpec((None, t, hd), lambda bi, i: (bi, 0, 2)),
                  pl.BlockSpec((None, tile, N_HEADS), lambda bi, i: (bi, i, 0)),
                  pl.BlockSpec((None, N_HEADS, nq, 1, tile), lambda bi, i: (bi, 0, 0, 0, 0))],
        out_specs=pl.BlockSpec((None, tile, hd), lambda bi, i: (bi, i, 0)),
        out_shape=jax.ShapeDtypeStruct((b, t, hd), BF16),
        compiler_params=_cparams(2),
    )(u_fox, u_fox, u_fox, c_t, c_s)


def _mla_attn_kernel(q_ref, lat_ref, wuv_ref, o_ref, *, tile, lora, scale):
    i = pl.program_id(1)
    wq = q_ref.shape[1] // N_HEADS
    q = jnp.concatenate([q_ref[:, h * wq:(h + 1) * wq] for h in range(N_HEADS)], axis=0)
    rows_all = N_HEADS * tile
    r = lax.broadcasted_iota(jnp.int32, (rows_all, tile), 0) % tile
    c = lax.broadcasted_iota(jnp.int32, (rows_all, tile), 1)
    mask = c <= r

    def step(j, carry, masked):
        lat = lat_ref[pl.ds(pl.multiple_of(j * tile, tile), tile), :]
        s = _nt(q, lat) * scale
        if masked:
            s = jnp.where(mask, s, NEG)
        return _softmax_step(carry, s, lat[:, :lora])

    init = (jnp.full((rows_all, 1), NEG, F32), jnp.zeros((rows_all, 1), F32), jnp.zeros((rows_all, lora), F32))
    carry = lax.fori_loop(0, i, lambda j, c: step(j, c, False), init)
    m, l, acc = step(i, carry, True)
    o = (acc / l).astype(BF16)
    o = jnp.concatenate([o[h * tile:(h + 1) * tile] for h in range(N_HEADS)], axis=-1)
    o_ref[...] = _dot(o, wuv_ref[...]).astype(o_ref.dtype)


def _mla_attn(q_cat, lat_pad, w_uv_bd, tile, lora, scale):
    b, t, wq = q_cat.shape
    wl = lat_pad.shape[-1]
    wo = w_uv_bd.shape[1]
    return pl.pallas_call(
        functools.partial(_mla_attn_kernel, tile=tile, lora=lora, scale=scale),
        grid=(b, t // tile),
        in_specs=[pl.BlockSpec((None, tile, wq), lambda bi, i: (bi, i, 0)),
                  pl.BlockSpec((None, t, wl), lambda bi, i: (bi, 0, 0)),
                  pl.BlockSpec(w_uv_bd.shape, lambda bi, i: (0, 0))],
        out_specs=pl.BlockSpec((None, tile, wo), lambda bi, i: (bi, i, 0)),
        out_shape=jax.ShapeDtypeStruct((b, t, wo), BF16),
        compiler_params=_cparams(2),
    )(q_cat, lat_pad, w_uv_bd)


def _suffix_matrix(n, with_total):
    j = lax.broadcasted_iota(jnp.int32, (n, n), 0)
    s = lax.broadcasted_iota(jnp.int32, (n, n), 1)
    m = jnp.where(j > s, 1.0, 0.0).astype(BF16)
    if with_total:
        m = jnp.concatenate([m, jnp.ones((n, n), BF16)], axis=1)
    return m


def _row_sum(x):
    return jnp.sum(x, axis=-1, keepdims=True)


def _sb_weights(z, carry_a, msuf, valid):
    ls = _log_sigmoid(z)
    lfail = ls - z
    if valid is not None:
        lfail = jnp.where(valid, lfail, 0.0)
    w = jnp.exp(ls + _dot_f32_lhs2(lfail, msuf) + carry_a)
    if valid is not None:
        w = jnp.where(valid, w, 0.0)
    return w, carry_a + _row_sum(lfail)


def _sb_block(z, v, carry_a, acc, msuf, valid):
    n = z.shape[1]
    ls = _log_sigmoid(z)
    lfail = ls - z
    if valid is not None:
        lfail = jnp.where(valid, lfail, 0.0)
    ct = _dot_f32_lhs2(lfail, msuf)
    w = jnp.exp(ls + ct[:, :n] + carry_a)
    if valid is not None:
        w = jnp.where(valid, w, 0.0)
    return carry_a + ct[:, n:], acc + _dot(w.astype(BF16), v)


def _sb_attn_kernel(q_ref, k_ref, v_ref, o_ref, *, tile, dh):
    i = pl.program_id(1)
    scale = dh ** -0.5
    valid = _causal_mask(tile, tile, True)
    msuf = _suffix_matrix(tile, True)
    heads = [slice(h * dh, (h + 1) * dh) for h in range(N_HEADS)]
    qs = [(q_ref[:, sl] * scale).astype(BF16) for sl in heads]

    def step(j, carry, mask):
        rows = pl.ds(pl.multiple_of(j * tile, tile), tile)
        return tuple(_sb_block(_nt(qs[h], k_ref[rows, sl].astype(BF16)), v_ref[rows, sl].astype(BF16),
                               carry[h][0], carry[h][1], msuf, mask) for h, sl in enumerate(heads))

    init = tuple((jnp.zeros((tile, tile), F32), jnp.zeros((tile, dh), F32)) for _ in heads)
    carry = step(i, init, valid)
    carry = lax.fori_loop(0, i, lambda t, c: step(i - 1 - t, c, None), carry)
    for sl, (a, acc) in zip(heads, carry):
        o_ref[:, sl] = acc.astype(o_ref.dtype)


def _sb_attn(u_sb, tile):
    b, t, w = u_sb.shape
    hd = w // 3
    dh = hd // N_HEADS
    return pl.pallas_call(
        functools.partial(_sb_attn_kernel, tile=tile, dh=dh),
        grid=(b, t // tile),
        in_specs=[pl.BlockSpec((None, tile, hd), lambda bi, i: (bi, i, 0)),
                  pl.BlockSpec((None, t, hd), lambda bi, i: (bi, 0, 1)),
                  pl.BlockSpec((None, t, hd), lambda bi, i: (bi, 0, 2))],
        out_specs=pl.BlockSpec((None, tile, hd), lambda bi, i: (bi, i, 0)),
        out_shape=jax.ShapeDtypeStruct((b, t, hd), BF16),
        compiler_params=_cparams(2),
    )(u_sb, u_sb, u_sb)


def _rms(x, g):
    return x * lax.rsqrt(jnp.mean(jnp.square(x), axis=-1, keepdims=True) + RMS_EPS) * g


def _mla_prep_kernel(u_ref, s_ref, cos_ref, sin_ref, gq_ref, gkv_ref, wn_ref, wr_ref, wrs_ref, wuk_ref,
                     qcat_ref, latpad_ref, lat_ref, *, q_lora, kv_lora, rope):
    cos = cos_ref[...]
    sin = sin_ref[...]
    cq = _rms(u_ref[:, :q_lora], gq_ref[...]).astype(BF16)
    q_nope = _dot(cq, wn_ref[...])
    q_rope = _dot(cq, wr_ref[...])
    q_rope_sw = _dot(cq, wrs_ref[...])
    q_abs = _dot(q_nope.astype(BF16), wuk_ref[...])
    pieces = []
    for h in range(N_HEADS):
        blk = slice(h * LANES, (h + 1) * LANES)
        pieces.append(q_abs[:, h * kv_lora:(h + 1) * kv_lora])
        pieces.append(q_rope[:, blk] * cos + q_rope_sw[:, blk] * sin)
    qcat_ref[...] = jnp.concatenate(pieces, axis=-1).astype(qcat_ref.dtype)
    ckv = _rms(u_ref[:, q_lora:q_lora + kv_lora], gkv_ref[...])
    small = s_ref[...]
    kr = small * cos + pltpu.roll(small, LANES - rope, 1) * sin
    lane = lax.broadcasted_iota(jnp.int32, kr.shape, 1)
    kr = jnp.where(lane < rope, kr, 0.0)
    latpad_ref[...] = jnp.concatenate([ckv, kr], axis=-1).astype(latpad_ref.dtype)
    lat_ref[...] = jnp.concatenate([ckv, kr[:, :rope]], axis=-1)


def _mla_prep(u_mla, small, cos, sin, g_q, g_kv, w_nope, w_rope, w_rope_sw, w_uk_bd, tm, rope):
    n, wm = u_mla.shape
    q_lora = g_q.shape[1]
    kv_lora = g_kv.shape[1]
    ncs = cos.shape[0] // tm
    wq = N_HEADS * (kv_lora + LANES)
    full = lambda a: pl.BlockSpec(a.shape, lambda i: (0, 0))
    return pl.pallas_call(
        functools.partial(_mla_prep_kernel, q_lora=q_lora, kv_lora=kv_lora, rope=rope),
        grid=(n // tm,),
        in_specs=[pl.BlockSpec((tm, wm), lambda i: (i, 0)),
                  pl.BlockSpec((tm, LANES), lambda i: (i, 0)),
                  pl.BlockSpec((tm, LANES), lambda i: (i % ncs, 0)),
                  pl.BlockSpec((tm, LANES), lambda i: (i % ncs, 0)),
                  full(g_q), full(g_kv), full(w_nope), full(w_rope), full(w_rope_sw), full(w_uk_bd)],
        out_specs=[pl.BlockSpec((tm, wq), lambda i: (i, 0)),
                   pl.BlockSpec((tm, kv_lora + LANES), lambda i: (i, 0)),
                   pl.BlockSpec((tm, kv_lora + rope), lambda i: (i, 0))],
        out_shape=[jax.ShapeDtypeStruct((n, wq), BF16),
                   jax.ShapeDtypeStruct((n, kv_lora + LANES), BF16),
                   jax.ShapeDtypeStruct((n, kv_lora + rope), F32)],
        compiler_params=_cparams(1),
    )(u_mla, small, cos, sin, g_q, g_kv, w_nope, w_rope, w_rope_sw, w_uk_bd)


def _page_specs(block, layer, n_pages, pps):
    zeros = (0,) * len(block)

    def spec(r):
        return pl.BlockSpec((None, None) + block,
                            lambda b, c, pt, r=r: (layer, pt[b * n_pages + (n_pages - 1 - (c * pps + r))]) + zeros)
    return [spec(r) for r in range(pps)]


def _new_token_mask(rows, n_new, strict):
    r = lax.broadcasted_iota(jnp.int32, (rows, LANES), 0) % (rows // N_HEADS)
    c = lax.broadcasted_iota(jnp.int32, (rows, LANES), 1)
    limit = jnp.minimum(r if strict else r + 1, n_new)
    return c < limit


def _head_scores(q, kv_refs):
    qr = q.shape[0] // N_HEADS
    ss, vs = [], []
    for h in range(N_HEADS):
        kt = jnp.concatenate([ref[0, h].astype(BF16) for ref in kv_refs], axis=1)
        ss.append(_dot(q[h * qr:(h + 1) * qr], kt))
        vs.append(jnp.concatenate([ref[1, h].astype(BF16) for ref in kv_refs], axis=1))
    return jnp.concatenate(ss, axis=0), vs


def _head_pv(p, vs):
    qr = p.shape[0] // N_HEADS
    return jnp.concatenate([_nt(p[h * qr:(h + 1) * qr].astype(BF16), vs[h]) for h in range(N_HEADS)], axis=0)


def _stage_new_rows(pad_ref, new_ref):
    pad_ref[...] = jnp.zeros(pad_ref.shape, F32)
    pad_ref[0:new_ref.shape[0], :] = new_ref[...]


def _head_scores_rows(q, pad_ref):
    qr = q.shape[0] // N_HEADS
    hd = pad_ref.shape[1] // 2
    dh = hd // N_HEADS
    ss = [_nt(q[h * qr:(h + 1) * qr], pad_ref[:, h * dh:(h + 1) * dh].astype(BF16)) for h in range(N_HEADS)]
    vs = [pad_ref[:, hd + h * dh:hd + (h + 1) * dh].astype(BF16) for h in range(N_HEADS)]
    return jnp.concatenate(ss, axis=0), vs


def _head_pv_rows(p, vs):
    qr = p.shape[0] // N_HEADS
    return jnp.concatenate([_dot(p[h * qr:(h + 1) * qr].astype(BF16), vs[h]) for h in range(N_HEADS)], axis=0)


def _fox_sfx_kernel(x_ref, m_ref, o_ref):
    o_ref[...] = _dot_f32_lhs(x_ref[...], m_ref[...])


def _fox_sfx(logf_rows):
    n, page = logf_rows.shape
    tr = 2048 if n % 2048 == 0 else n
    m = _suffix_matrix(page, True)
    return pl.pallas_call(
        _fox_sfx_kernel,
        grid=(n // tr,),
        in_specs=[pl.BlockSpec((tr, page), lambda i: (i, 0)), pl.BlockSpec(m.shape, lambda i: (0, 0))],
        out_specs=pl.BlockSpec((tr, 2 * page), lambda i: (i, 0)),
        out_shape=jax.ShapeDtypeStruct((n, 2 * page), F32),
        compiler_params=_cparams(1),
    )(logf_rows, m)


def _softmax_update(m_ref, l_ref, acc_ref, s, pv, first):
    smax = jnp.max(s, axis=-1, keepdims=True)
    if first:
        m_new = smax
    else:
        m_old = m_ref[...]
        m_new = jnp.maximum(m_old, smax)
        alpha = jnp.exp(m_old - m_new)
    p = jnp.exp(s - m_new)
    l = jnp.sum(p, axis=-1, keepdims=True)
    acc = pv(p)
    if not first:
        l = l + alpha * l_ref[...]
        acc = acc + alpha * acc_ref[...]
    m_ref[...] = m_new
    l_ref[...] = l
    acc_ref[...] = acc
    return l, acc


def _fox_dec_kernel(pt_ref, q_ref, kvn_ref, gn_ref, *rest, pps, n_new):
    kv_refs, sfx_refs = rest[:pps], rest[pps:2 * pps]
    o_ref, m_ref, l_ref, acc_ref, car_ref, pad_ref = rest[2 * pps:]
    c = pl.program_id(1)
    q = q_ref[...]
    rows = q.shape[0]
    qr = rows // N_HEADS

    @pl.when(c == 0)
    def _():
        _stage_new_rows(pad_ref, kvn_ref)
        s, vs = _head_scores_rows(q, pad_ref)
        s = jnp.where(_new_token_mask(rows, n_new, False), s + gn_ref[...], NEG)
        _softmax_update(m_ref, l_ref, acc_ref, s, lambda p: _head_pv_rows(p, vs), True)
        car_ref[...] = jnp.zeros(car_ref.shape, F32)

    car = car_ref[0:N_HEADS, :]
    biases = []
    for r in range(pps):
        blk = sfx_refs[r][...]
        biases.append(blk[:, :LANES] + car)
        car = car + blk[:, LANES:]
    car_ref[0:N_HEADS, :] = car
    bias = jnp.concatenate(biases, axis=1)
    s, vs = _head_scores(q, kv_refs)
    s = s + jnp.concatenate([jnp.broadcast_to(bias[h:h + 1, :], (qr, bias.shape[1])) for h in range(N_HEADS)], axis=0)
    l, acc = _softmax_update(m_ref, l_ref, acc_ref, s, lambda p: _head_pv(p, vs), False)

    @pl.when(c == pl.num_programs(1) - 1)
    def _():
        o_ref[...] = acc / l


def _fox_dec(layer, q_rows, kv_new, g_new, cache_kv, sfx, page_table, n_new):
    db, rows, dh = q_rows.shape
    n_pages = page_table.shape[1]
    pps = min(PAGES_PER_STEP, n_pages)
    kv_block = cache_kv.shape[2:]
    sfx_block = sfx.shape[2:]
    grid_spec = pltpu.PrefetchScalarGridSpec(
        num_scalar_prefetch=1,
        grid=(db, n_pages // pps),
        in_specs=[pl.BlockSpec((None, rows, dh), lambda b, c, pt: (b, 0, 0)),
                  pl.BlockSpec((None,) + kv_new.shape[1:], lambda b, c, pt: (b, 0, 0)),
                  pl.BlockSpec((None, rows, LANES), lambda b, c, pt: (b, 0, 0))]
        + _page_specs(kv_block, layer, n_pages, pps) + _page_specs(sfx_block, layer, n_pages, pps),
        out_specs=pl.BlockSpec((None, rows, dh), lambda b, c, pt: (b, 0, 0)),
        scratch_shapes=[pltpu.VMEM((rows, 1), F32), pltpu.VMEM((rows, 1), F32), pltpu.VMEM((rows, dh), F32),
                        pltpu.VMEM((SUBLANES, LANES), F32), pltpu.VMEM((kv_block[-1], kv_new.shape[2]), F32)])
    return pl.pallas_call(
        functools.partial(_fox_dec_kernel, pps=pps, n_new=n_new),
        grid_spec=grid_spec,
        out_shape=jax.ShapeDtypeStruct((db, rows, dh), F32),
        compiler_params=_cparams(2),
    )(page_table.reshape(-1), q_rows, kv_new, g_new, *([cache_kv] * pps), *([sfx] * pps))


def _mla_dec_kernel(pt_ref, q_ref, latn_ref, wuv_ref, *rest, pps, n_new, lora, scale):
    lat_refs = rest[:pps]
    o_ref, m_ref, l_ref, acc_ref, pad_ref = rest[pps:]
    c = pl.program_id(1)
    q = q_ref[...]
    rows = q.shape[0]

    @pl.when(c == 0)
    def _():
        _stage_new_rows(pad_ref, latn_ref)
        lat = pad_ref[...].astype(BF16)
        s = jnp.where(_new_token_mask(rows, n_new, False), _nt(q, lat) * scale, NEG)
        _softmax_update(m_ref, l_ref, acc_ref, s, lambda p: _dot(p.astype(BF16), lat[:, :lora]), True)

    lat = jnp.concatenate([lat_refs[r][...].astype(BF16) for r in range(pps)], axis=1)
    l, acc = _softmax_update(m_ref, l_ref, acc_ref, _dot(q, lat) * scale,
                             lambda p: _nt(p.astype(BF16), lat[:lora]), False)

    @pl.when(c == pl.num_programs(1) - 1)
    def _():
        o_ref[...] = _dot((acc / l).astype(BF16), wuv_ref[...])


def _mla_dec(layer, q_rows, lat_new, w_uv_all, cache_lat, page_table, n_new, lora, scale):
    db, rows, wq = q_rows.shape
    n_pages = page_table.shape[1]
    pps = min(PAGES_PER_STEP, n_pages)
    lat_block = cache_lat.shape[2:]
    wo = w_uv_all.shape[1]
    grid_spec = pltpu.PrefetchScalarGridSpec(
        num_scalar_prefetch=1,
        grid=(db, n_pages // pps),
        in_specs=[pl.BlockSpec((None, rows, wq), lambda b, c, pt: (b, 0, 0)),
                  pl.BlockSpec((None,) + lat_new.shape[1:], lambda b, c, pt: (b, 0, 0)),
                  pl.BlockSpec(w_uv_all.shape, lambda b, c, pt: (0, 0))]
        + _page_specs(lat_block, layer, n_pages, pps),
        out_specs=pl.BlockSpec((None, rows, wo), lambda b, c, pt: (b, 0, 0)),
        scratch_shapes=[pltpu.VMEM((rows, 1), F32), pltpu.VMEM((rows, 1), F32), pltpu.VMEM((rows, lora), F32),
                        pltpu.VMEM((lat_block[-1], lat_new.shape[2]), F32)])
    return pl.pallas_call(
        functools.partial(_mla_dec_kernel, pps=pps, n_new=n_new, lora=lora, scale=scale),
        grid_spec=grid_spec,
        out_shape=jax.ShapeDtypeStruct((db, rows, wo), F32),
        compiler_params=_cparams(2),
    )(page_table.reshape(-1), q_rows, lat_new, w_uv_all, *([cache_lat] * pps))


def _sb_dec_kernel(pt_ref, q_ref, kvn_ref, *rest, pps, n_new):
    kv_refs = rest[:pps]
    o_ref, a_ref, acc_ref, pad_ref = rest[pps:]
    c = pl.program_id(1)
    q = q_ref[...]
    rows = q.shape[0]
    msuf = _suffix_matrix(LANES, False)

    @pl.when(c == 0)
    def _():
        _stage_new_rows(pad_ref, kvn_ref)
        z, vs = _head_scores_rows(q, pad_ref)
        w, a = _sb_weights(z, jnp.zeros(a_ref.shape, F32), msuf, _new_token_mask(rows, n_new, True))
        a_ref[...] = a
        acc_ref[...] = _head_pv_rows(w, vs)

    z, vs = _head_scores(q, kv_refs)
    ls = _log_sigmoid(z)
    lfail = ls - z
    pages = [lfail[:, r * LANES:(r + 1) * LANES] for r in range(pps)]
    ct = _dot_f32_lhs2(jnp.concatenate(pages, axis=0), msuf)
    a = a_ref[...]
    cums = []
    for r in range(pps):
        cums.append(ct[r * rows:(r + 1) * rows] + a)
        a = a + _row_sum(pages[r])
    a_ref[...] = a
    acc = acc_ref[...] + _head_pv(jnp.exp(ls + jnp.concatenate(cums, axis=1)), vs)
    acc_ref[...] = acc

    @pl.when(c == pl.num_programs(1) - 1)
    def _():
        o_ref[...] = acc


def _sb_dec(layer, q_rows, kv_new, cache_kv, page_table, n_new):
    db, rows, dh = q_rows.shape
    n_pages = page_table.shape[1]
    pps = min(PAGES_PER_STEP, n_pages)
    kv_block = cache_kv.shape[2:]
    grid_spec = pltpu.PrefetchScalarGridSpec(
        num_scalar_prefetch=1,
        grid=(db, n_pages // pps),
        in_specs=[pl.BlockSpec((None, rows, dh), lambda b, c, pt: (b, 0, 0)),
                  pl.BlockSpec((None,) + kv_new.shape[1:], lambda b, c, pt: (b, 0, 0))]
        + _page_specs(kv_block, layer, n_pages, pps),
        out_specs=pl.BlockSpec((None, rows, dh), lambda b, c, pt: (b, 0, 0)),
        scratch_shapes=[pltpu.VMEM((rows, 1), F32), pltpu.VMEM((rows, dh), F32),
                        pltpu.VMEM((kv_block[-1], kv_new.shape[2]), F32)])
    return pl.pallas_call(
        functools.partial(_sb_dec_kernel, pps=pps, n_new=n_new),
        grid_spec=grid_spec,
        out_shape=jax.ShapeDtypeStruct((db, rows, dh), F32),
        compiler_params=_cparams(2),
    )(page_table.reshape(-1), q_rows, kv_new, *([cache_kv] * pps))


def _layer_norm(v, g, b):
    mu = jnp.mean(v, axis=-1, keepdims=True)
    var = jnp.mean(jnp.square(v - mu), axis=-1, keepdims=True)
    return (v - mu) * lax.rsqrt(var + LN_EPS) * g + b


def _merge_kernel(x_ref, mod_ref, ba_ref, bb_ref, bc_ref, bd_ref, wg_ref, wb_ref, wo_ref, g_ref, b_ref,
                  wr_ref, br_ref, cnt0_ref, x1_ref, h2_ref, ti_ref, tp_ref, rk_ref, cnt_ref, *, alpha):
    @pl.when(pl.program_id(0) == 0)
    def _():
        cnt_ref[...] = cnt0_ref[...]

    x = x_ref[...]
    tm, d = x.shape
    h = (x * (1.0 + mod_ref[1]) + mod_ref[0]).astype(BF16)
    y = None
    for n, br in enumerate((ba_ref, bb_ref, bc_ref, bd_ref)):
        gate = jax.nn.sigmoid(_dot(h, wg_ref[:, n * d:(n + 1) * d]))
        term = gate * _dot(br[...], wb_ref[n])
        y = term if y is None else y + term
    y = _dot(y.astype(BF16), wo_ref[...])
    x1 = _layer_norm(alpha * x + (1.0 + mod_ref[2]) * y, g_ref[...], b_ref[...])
    x1_ref[...] = x1
    h2 = x1 * (1.0 + mod_ref[4]) + mod_ref[3]
    h2_ref[...] = h2
    logits = _dot(h2.astype(BF16), wr_ref[...]) + br_ref[...]
    ne = logits.shape[1]
    lane = lax.broadcasted_iota(jnp.int32, logits.shape, 1).astype(F32)
    kcol = lax.broadcasted_iota(jnp.int32, (tm, TOP_K), 1)
    tv = jnp.zeros((tm, TOP_K), F32)
    ti = jnp.zeros((tm, TOP_K), F32)
    picks = []
    for k in range(TOP_K):
        m = jnp.max(logits, axis=-1, keepdims=True)
        idx = jnp.min(jnp.where(logits == m, lane, float(ne)), axis=-1, keepdims=True)
        pick = lane == idx
        picks.append(pick)
        tv = jnp.where(kcol == k, m, tv)
        ti = jnp.where(kcol == k, idx, ti)
        logits = jnp.where(pick, -jnp.inf, logits)
    e = jnp.exp(tv - tv[:, 0:1])
    ti_ref[...] = ti.astype(jnp.int32)
    tp_ref[...] = e / jnp.sum(e, axis=-1, keepdims=True)
    chosen = None
    for pick in picks:
        one = jnp.where(pick, 1.0, 0.0)
        chosen = one if chosen is None else chosen + one
    r = lax.broadcasted_iota(jnp.int32, (tm, tm), 0)
    c = lax.broadcasted_iota(jnp.int32, (tm, tm), 1)
    earlier = jnp.where(c < r, 1.0, 0.0).astype(BF16)
    base = cnt_ref[...] + _dot(earlier, chosen.astype(BF16))
    rk = jnp.zeros((tm, TOP_K), F32)
    for k, pick in enumerate(picks):
        rk = jnp.where(kcol == k, jnp.sum(jnp.where(pick, base, 0.0), axis=-1, keepdims=True), rk)
    rk_ref[...] = rk.astype(jnp.int32)
    cnt_ref[...] = cnt_ref[...] + jnp.sum(chosen, axis=0, keepdims=True)


def _merge(x2d, mod, branches, w_gate, w_branch, w_out, ln_g, ln_b, w_router, b_router, counts, tm, rows_per_seq, alpha):
    n, d = x2d.shape
    full = lambda a: pl.BlockSpec(a.shape, lambda i: (0,) * a.ndim)
    row = lambda w: pl.BlockSpec((tm, w), lambda i: (i, 0))
    return pl.pallas_call(
        functools.partial(_merge_kernel, alpha=alpha),
        grid=(n // tm,),
        in_specs=[row(d), _mod_spec(mod, tm, rows_per_seq)] + [row(b.shape[1]) for b in branches]
        + [full(w_gate), full(w_branch), full(w_out), full(ln_g), full(ln_b), full(w_router), full(b_router),
           full(counts)],
        out_specs=[row(d), row(d), row(TOP_K), row(TOP_K), row(TOP_K), full(counts)],
        out_shape=[jax.ShapeDtypeStruct((n, d), F32), jax.ShapeDtypeStruct((n, d), F32),
                   jax.ShapeDtypeStruct((n, TOP_K), jnp.int32), jax.ShapeDtypeStruct((n, TOP_K), F32),
                   jax.ShapeDtypeStruct((n, TOP_K), jnp.int32), jax.ShapeDtypeStruct(counts.shape, F32)],
        compiler_params=_cparams(1),
    )(x2d, mod, *branches, w_gate, w_branch, w_out, ln_g, ln_b, w_router, b_router, counts)


def _expert_kernel(be_ref, nb_ref, x_ref, w1_ref, b1g_ref, b1l_ref, w2_ref, b2_ref, o_ref,
                   wt_ref, wg_ref, wl_ref, w2b_ref):
    i = pl.program_id(0)
    f = wg_ref.shape[0]

    @pl.when((i == 0) | (be_ref[i] != be_ref[jnp.maximum(i - 1, 0)]))
    def _():
        for c in range(wt_ref.shape[0]):
            cols = slice(c * LANES, (c + 1) * LANES)
            wt_ref[c] = w1_ref[cols, :].T
            wg_ref[:, cols] = wt_ref[c, pl.ds(0, f, stride=2), :].astype(BF16)
            wl_ref[:, cols] = wt_ref[c, pl.ds(1, f, stride=2), :].astype(BF16)
        w2b_ref[...] = w2_ref[...].astype(BF16)

    @pl.when(i < nb_ref[0])
    def _():
        x = x_ref[...].astype(BF16)
        glu = jnp.minimum(_nt(x, wg_ref[...]) + b1g_ref[...], SWIGLU_LIMIT)
        lin = jnp.clip(_nt(x, wl_ref[...]) + b1l_ref[...], -SWIGLU_LIMIT, SWIGLU_LIMIT)
        act = glu * jax.nn.sigmoid(SWIGLU_ALPHA * glu) * (lin + 1.0)
        o_ref[...] = _dot(act.astype(BF16), w2b_ref[...]) + b2_ref[...]


def _experts(layer, block_exp, n_used, xs, w1, b1g, b1l, w2, b2):
    n_slots, d = xs.shape
    f2 = w1.shape[-1]
    f = f2 // 2
    nb = n_slots // MOE_ROWS
    blk = lambda i, be, nu: (jnp.minimum(i, nu[0] - 1), 0)
    expert = lambda i, be, nu: be[jnp.minimum(i, nu[0] - 1)]
    wspec = lambda s: pl.BlockSpec((None, None) + s, lambda i, be, nu: (layer, expert(i, be, nu), 0, 0))
    bspec = lambda w: pl.BlockSpec((None, 1, w), lambda i, be, nu: (expert(i, be, nu), 0, 0))
    grid_spec = pltpu.PrefetchScalarGridSpec(
        num_scalar_prefetch=2,
        grid=(nb,),
        in_specs=[pl.BlockSpec((MOE_ROWS, d), blk), wspec((d, f2)), bspec(f), bspec(f), wspec((f, d)), bspec(d)],
        out_specs=pl.BlockSpec((MOE_ROWS, d), blk),
        scratch_shapes=[pltpu.VMEM((d // LANES, f2, LANES), F32), pltpu.VMEM((f, d), BF16), pltpu.VMEM((f, d), BF16),
                        pltpu.VMEM((f, d), BF16)])
    return pl.pallas_call(
        _expert_kernel,
        grid_spec=grid_spec,
        out_shape=jax.ShapeDtypeStruct((n_slots, d), F32),
        compiler_params=_cparams(1),
    )(block_exp, n_used, xs, w1, b1g, b1l, w2, b2)


def _final_kernel(x_ref, mod_ref, *rest, alpha):
    y_refs, (p_ref, g_ref, b_ref, o_ref) = rest[:TOP_K], rest[TOP_K:]
    p = p_ref[...]
    f = None
    for k in range(TOP_K):
        term = p[:, k:k + 1] * y_refs[k][...]
        f = term if f is None else f + term
    o_ref[...] = _layer_norm(alpha * x_ref[...] + (1.0 + mod_ref[5]) * f, g_ref[...], b_ref[...])


def _final(x1, mod, ys, probs, ln_g, ln_b, tm, rows_per_seq, alpha):
    n, d = x1.shape
    full = lambda a: pl.BlockSpec(a.shape, lambda i: (0,) * a.ndim)
    return pl.pallas_call(
        functools.partial(_final_kernel, alpha=alpha),
        grid=(n // tm,),
        in_specs=[pl.BlockSpec((tm, d), lambda i: (i, 0)), _mod_spec(mod, tm, rows_per_seq)]
        + [pl.BlockSpec((tm, d), lambda i: (i, 0))] * TOP_K
        + [pl.BlockSpec((tm, TOP_K), lambda i: (i, 0)), full(ln_g), full(ln_b)],
        out_specs=pl.BlockSpec((tm, d), lambda i: (i, 0)),
        out_shape=jax.ShapeDtypeStruct((n, d), F32),
        compiler_params=_cparams(1),
    )(x1, mod, *ys, probs, ln_g, ln_b)


def _rope_tables(pos, rope):
    half = rope // 2
    inv = ROPE_BASE ** (-jnp.arange(half, dtype=F32) / half)
    ang = pos.astype(F32)[:, None] * inv[None, :]
    cos, sin = jnp.cos(ang), jnp.sin(ang)
    reps = LANES // rope
    return (jnp.tile(jnp.concatenate([cos, cos], -1), (1, reps)),
            jnp.tile(jnp.concatenate([-sin, sin], -1), (1, reps)))


def _swap_halves(w):
    half = w.shape[-1] // 2
    return jnp.concatenate([w[..., half:], w[..., :half]], axis=-1)


def _layer_weights(l, p, dims):
    d, dp, hd, q_lora, kv_lora, rope = dims
    w_a, w_gate = _win_layout(p['w_in'][l], dims)
    eye = jnp.eye(N_HEADS, dtype=F32)
    w_uq = p['w_mla_uq'][l]
    nope = w_uq.shape[-1] - rope
    pad = lambda w: jnp.pad(w, ((0, 0), (0, 0), (0, LANES - rope))).reshape(q_lora, N_HEADS * LANES)
    w_rope = w_uq[:, :, nope:]
    lw = {
        'w_a': w_a,
        'w_gate': w_gate,
        'w_pool_bd': jnp.einsum('gcd,gh->gchd', p['w_pool_mix'][l], jnp.eye(len(POOL_WINDOWS), dtype=F32)
                                ).reshape(dp, dp).astype(BF16),
        'pool_scale': p['pool_scale'][l].reshape(1, dp),
        'fox_bias': jnp.zeros((1, LANES), F32).at[0, 2 * rope:2 * rope + N_HEADS].set(p['b_fox_forget'][l]),
        'g_q': p['mla_q_norm'][l].reshape(1, q_lora),
        'g_kv': p['mla_kv_norm'][l].reshape(1, kv_lora),
        'w_nope': w_uq[:, :, :nope].reshape(q_lora, N_HEADS * nope).astype(BF16),
        'w_rope': pad(w_rope).astype(BF16),
        'w_rope_sw': pad(_swap_halves(w_rope)).astype(BF16),
        'w_uk_bd': jnp.einsum('chn,hg->hngc', p['w_mla_uk'][l], eye).reshape(N_HEADS * nope, N_HEADS * kv_lora).astype(BF16),
        'w_uv_bd': jnp.einsum('chv,hg->hcgv', p['w_mla_uv'][l], eye).reshape(N_HEADS * kv_lora, -1).astype(BF16),
        'w_uv_all': p['w_mla_uv'][l].reshape(kv_lora, -1).astype(BF16),
        'w_branch': p['w_branch'][l].astype(BF16),
        'w_out': p['w_out'][l].astype(BF16),
        'ln1_g': p['ln1_g'][l].reshape(1, d), 'ln1_b': p['ln1_b'][l].reshape(1, d),
        'ln2_g': p['ln2_g'][l].reshape(1, d), 'ln2_b': p['ln2_b'][l].reshape(1, d),
        'b_router': p['b_router'][l].reshape(1, -1),
        'b1g': p['b_exp1'][l][:, None, 0::2], 'b1l': p['b_exp1'][l][:, None, 1::2],
        'b2': p['b_exp2'][l][:, None, :],
        'w_router': p['w_router'][l].astype(BF16),
    }
    lw['widths'] = (dp, 3 * hd, q_lora + kv_lora, 3 * hd, LANES)
    lw['nope'] = nope
    return lw


def _routing(top_i, rank, counts):
    n = top_i.shape[0]
    nk = n * TOP_K
    n_experts = counts.shape[-1]
    sizes = counts.reshape(n_experts).astype(jnp.int32)
    padded = (sizes + MOE_ROWS - 1) // MOE_ROWS * MOE_ROWS
    ends = jnp.cumsum(padded)
    starts = ends - padded
    first = jnp.cumsum(sizes) - sizes
    dest = starts[top_i] + rank
    n_blocks = -(-(nk + n_experts * (MOE_ROWS - 1)) // MOE_ROWS)
    blk_start = jnp.arange(n_blocks, dtype=jnp.int32) * MOE_ROWS
    block_exp = jnp.minimum(jnp.sum(ends[None, :] <= blk_start[:, None], axis=1), n_experts - 1).astype(jnp.int32)
    bits = max(nk - 1, 1).bit_length()
    keys = jnp.sort(top_i.reshape(nk) * (1 << bits) + jnp.arange(nk, dtype=jnp.int32))
    order = keys & ((1 << bits) - 1)
    r = (blk_start - starts[block_exp])[:, None] + jnp.arange(MOE_ROWS, dtype=jnp.int32)[None, :]
    pair = order[jnp.clip(first[block_exp][:, None] + r, 0, nk - 1).reshape(-1)]
    slot_tok = jnp.where((r < sizes[block_exp][:, None]).reshape(-1), pair // TOP_K, 0).astype(jnp.int32)
    n_used = (ends[-1] // MOE_ROWS).astype(jnp.int32).reshape(1)
    return slot_tok, dest, block_exp, n_used


def _head_rows(q, scale):
    db, dt, hd = q.shape
    dh = hd // N_HEADS
    q4 = jnp.transpose((q * scale).reshape(db, dt, N_HEADS, dh), (0, 2, 1, 3))
    q4 = jnp.pad(q4, ((0, 0), (0, 0), (0, DEC_ROWS - dt), (0, 0)))
    return q4.reshape(db, N_HEADS * DEC_ROWS, dh).astype(BF16)


def _head_out(o, dt):
    db, _, dv = o.shape
    o4 = jnp.transpose(o.reshape(db, N_HEADS, DEC_ROWS, dv)[:, :, :dt], (0, 2, 1, 3))
    return o4.reshape(db * dt, N_HEADS * dv).astype(BF16)


def _diag_heads(o, dt):
    db, _, hv = o.shape
    dv = hv // N_HEADS
    o5 = o.reshape(db, N_HEADS, dt, N_HEADS, dv)
    d = jnp.stack([o5[:, h, :, h, :] for h in range(N_HEADS)], axis=2)
    return d.reshape(db * dt, hv).astype(BF16)


def kernel(x_prompt, x_sample, c_prompt, c_sample, cache_fox_kv, cache_fox_logf, cache_mla, cache_sb_kv, state_pool, page_table, w_ada, b_ada, w_in, b_fox_forget, w_pool_mix, pool_scale, mla_q_norm, w_mla_uq, mla_kv_norm, w_mla_uk, w_mla_uv, w_branch, w_out, ln1_g, ln1_b, w_router, b_router, w_exp1, b_exp1, w_exp2, b_exp2, ln2_g, ln2_b):
    p = dict(w_in=w_in, b_fox_forget=b_fox_forget, w_pool_mix=w_pool_mix, pool_scale=pool_scale,
             mla_q_norm=mla_q_norm, w_mla_uq=w_mla_uq, mla_kv_norm=mla_kv_norm, w_mla_uk=w_mla_uk,
             w_mla_uv=w_mla_uv, w_branch=w_branch, w_out=w_out, ln1_g=ln1_g, ln1_b=ln1_b, w_router=w_router,
             b_router=b_router, w_exp1=w_exp1, b_exp1=b_exp1, w_exp2=w_exp2, b_exp2=b_exp2, ln2_g=ln2_g, ln2_b=ln2_b)
    b, t, d = x_prompt.shape
    db, dt, _ = x_sample.shape
    depth = w_ada.shape[0]
    n_pages = page_table.shape[1]
    page = cache_fox_kv.shape[2]
    past = n_pages * page
    dp = state_pool.shape[-1]
    hd = cache_fox_kv.shape[-1] * cache_fox_kv.shape[-2]
    dh = hd // N_HEADS
    q_lora = mla_q_norm.shape[1]
    kv_lora = mla_kv_norm.shape[1]
    rope = cache_mla.shape[-1] - kv_lora
    n_experts = w_router.shape[-1]
    alpha = (2 * depth) ** 0.25
    dims = (d, dp, hd, q_lora, kv_lora, rope)
    np_rows, ns_rows = b * t, db * dt
    tm = min(ROW_TILE, t)
    tile = min(ATT_TILE, t)
    sb_tile = min(SB_TILE, t)
    nq = t // tile

    mods = _ada(jnp.concatenate([c_prompt, c_sample], axis=0), w_ada, b_ada)
    cos_p, sin_p = _rope_tables(jnp.arange(t), rope)
    cos_s, sin_s = (jnp.tile(a, (db, 1)) for a in _rope_tables(past + jnp.arange(dt), rope))
    fox_kv_t = jnp.transpose(cache_fox_kv, (0, 1, 3, 4, 5, 2))
    sb_kv_t = jnp.transpose(cache_sb_kv, (0, 1, 3, 4, 5, 2))
    mla_t = jnp.transpose(cache_mla, (0, 1, 3, 2))
    logf_t = jnp.transpose(cache_fox_logf, (0, 1, 3, 2))
    sfx = _fox_sfx(logf_t.reshape(-1, page)).reshape(logf_t.shape[:3] + (2 * page,))
    zero_counts = jnp.zeros((1, n_experts), F32)

    xp = x_prompt.reshape(np_rows, d)
    xs = x_sample.reshape(ns_rows, d)
    outs = {k: [] for k in ('fox_p', 'fox_s', 'lf_p', 'lf_s', 'mla_p', 'mla_s', 'sb_p', 'sb_s', 'pool_p', 'pool_s')}
    for l in range(depth):
        lw = _layer_weights(l, p, dims)
        mod_p = jnp.transpose(mods[l, :, :b], (1, 0, 2))[:, :, None, :]
        mod_s = jnp.repeat(mods[l, :, b:], dt, axis=1)[None]
        mla_scale = (lw['nope'] + rope) ** -0.5

        u_pool, u_fox, u_mla, u_sb, u_small = _inproj(xp, mod_p, lw['w_a'], lw['widths'], tm, t)
        br_a, pool_new = _pool(u_pool.reshape(b, t, dp), jnp.zeros((b, POOL_BUF, dp), F32),
                               lw['w_pool_bd'], lw['pool_scale'], 0)
        lf, cum = _fox_prep(u_small.reshape(b, t, LANES), lw['fox_bias'], 2 * rope)
        c_s = jnp.transpose(cum, (0, 2, 1)).reshape(b, N_HEADS, nq, 1, tile)
        br_b = _fox_attn(u_fox.reshape(b, t, 3 * hd), cum, c_s, tile)
        q_cat, lat_pad, lat_new = _mla_prep(u_mla, u_small, cos_p, sin_p, lw['g_q'], lw['g_kv'], lw['w_nope'],
                                            lw['w_rope'], lw['w_rope_sw'], lw['w_uk_bd'], tm, rope)
        br_c = _mla_attn(q_cat.reshape(b, t, -1), lat_pad.reshape(b, t, -1), lw['w_uv_bd'], tile, kv_lora, mla_scale)
        br_d = _sb_attn(u_sb.reshape(b, t, 3 * hd), sb_tile)
        branches = [br_a.reshape(np_rows, dp), br_b.reshape(np_rows, hd), br_c.reshape(np_rows, hd),
                    br_d.reshape(np_rows, hd)]
        x1_p, h2_p, ti_p, tp_p, rk_p, counts = _merge(
            xp, mod_p, branches, lw['w_gate'], lw['w_branch'], lw['w_out'], lw['ln1_g'], lw['ln1_b'],
            lw['w_router'], lw['b_router'], zero_counts, tm, t, alpha)
        outs['fox_p'].append(u_fox[:, hd:].reshape(b, t, 2, N_HEADS, dh))
        outs['lf_p'].append(lf)
        outs['mla_p'].append(lat_new.reshape(b, t, kv_lora + rope))
        outs['sb_p'].append(u_sb[:, hd:].reshape(b, t, 2, N_HEADS, dh))
        outs['pool_p'].append(pool_new)

        u_pool, u_fox, u_mla, u_sb, u_small = _inproj(xs, mod_s, lw['w_a'], lw['widths'], ns_rows, dt)
        br_a, pool_new = _pool(u_pool.reshape(db, dt, dp), state_pool[l], lw['w_pool_bd'], lw['pool_scale'], past)
        lf, cum = _fox_prep(u_small.reshape(db, dt, LANES), lw['fox_bias'], 2 * rope)
        u_fox3 = u_fox.reshape(db, dt, 3 * hd)
        g_new = jnp.broadcast_to(-jnp.transpose(cum, (0, 2, 1))[:, :, None, :], (db, N_HEADS, DEC_ROWS, dt))
        g_new = jnp.pad(g_new.reshape(db, N_HEADS * DEC_ROWS, dt), ((0, 0), (0, 0), (0, LANES - dt)))
        o_fox = _fox_dec(l, _head_rows(u_fox3[:, :, :hd], dh ** -0.5), u_fox3[:, :, hd:], g_new,
                         fox_kv_t, sfx, page_table, dt)
        q_cat, lat_pad, lat_new = _mla_prep(u_mla, u_small, cos_s, sin_s, lw['g_q'], lw['g_kv'], lw['w_nope'],
                                            lw['w_rope'], lw['w_rope_sw'], lw['w_uk_bd'], ns_rows, rope)
        wq = kv_lora + LANES
        q_rows = q_cat.reshape(db, dt, N_HEADS, wq)[..., :kv_lora + rope]
        q_rows = jnp.transpose(q_rows, (0, 2, 1, 3)).reshape(db, N_HEADS * dt, kv_lora + rope)
        o_mla = _mla_dec(l, q_rows, lat_new.reshape(db, dt, -1), lw['w_uv_all'], mla_t, page_table, dt, kv_lora,
                         mla_scale)
        u_sb3 = u_sb.reshape(db, dt, 3 * hd)
        o_sb = _sb_dec(l, _head_rows(u_sb3[:, :, :hd], dh ** -0.5), u_sb3[:, :, hd:],
                       sb_kv_t, page_table, dt)
        branches = [br_a.reshape(ns_rows, dp), _head_out(o_fox, dt), _diag_heads(o_mla, dt), _head_out(o_sb, dt)]
        x1_s, h2_s, ti_s, tp_s, rk_s, counts = _merge(
            xs, mod_s, branches, lw['w_gate'], lw['w_branch'], lw['w_out'], lw['ln1_g'], lw['ln1_b'],
            lw['w_router'], lw['b_router'], counts, ns_rows, dt, alpha)
        outs['fox_s'].append(u_fox3[:, :, hd:].reshape(db, dt, 2, N_HEADS, dh))
        outs['lf_s'].append(lf)
        outs['mla_s'].append(lat_new.reshape(db, dt, kv_lora + rope))
        outs['sb_s'].append(u_sb3[:, :, hd:].reshape(db, dt, 2, N_HEADS, dh))
        outs['pool_s'].append(pool_new)

        h2 = jnp.concatenate([h2_p, h2_s], axis=0)
        slot_tok, dest, block_exp, n_used = _routing(jnp.concatenate([ti_p, ti_s], axis=0),
                                                     jnp.concatenate([rk_p, rk_s], axis=0), counts)
        y_slots = _experts(l, block_exp, n_used, h2[slot_tok], w_exp1, lw['b1g'], lw['b1l'], w_exp2, lw['b2'])
        y_p = [y_slots[dest[:np_rows, k]] for k in range(TOP_K)]
        y_s = [y_slots[dest[np_rows:, k]] for k in range(TOP_K)]
        xp = _final(x1_p, mod_p, y_p, tp_p, lw['ln2_g'], lw['ln2_b'], tm, t, alpha)
        xs = _final(x1_s, mod_s, y_s, tp_s, lw['ln2_g'], lw['ln2_b'], ns_rows, dt, alpha)

    st = lambda k: jnp.stack(outs[k])
    return (xp.reshape(b, t, d), xs.reshape(db, dt, d), st('fox_p'), st('fox_s'), st('lf_p'), st('lf_s'),
            st('mla_p'), st('mla_s'), st('sb_p'), st('sb_s'), st('pool_p'), st('pool_s'))
```

```python
import functools

import jax
import jax.numpy as jnp
from jax import lax
from jax.experimental import pallas as pl
from jax.experimental.pallas import tpu as pltpu

F32 = jnp.float32
BF16 = jnp.bfloat16

POOL_WINDOWS = (2, 4, 8, 16)
POOL_BUF = max(POOL_WINDOWS) - 1
N_HEADS = 4
N_BRANCH = 4
N_MOD = 6
TOP_K = 4
ROPE_BASE = 10000.0
SWIGLU_ALPHA = 1.702
SWIGLU_LIMIT = 7.0
LN_EPS = 1e-5
RMS_EPS = 1e-6

LANES = 128
SUBLANES = 8
VMEM_LIMIT = 56 * 1024 * 1024

NEG = -1e30

ROW_TILE = 512
ATT_TILE = 512
SB_TILE = 256
PAGES_PER_STEP = 32
MOE_ROWS = 256
DEC_ROWS = 16


def _cparams(n_axes):
    return pltpu.CompilerParams(dimension_semantics=("arbitrary",) * n_axes, vmem_limit_bytes=VMEM_LIMIT)


def _nt(a, b):
    return lax.dot_general(a, b, (((1,), (1,)), ((), ())), preferred_element_type=F32)


def _dot(a, b):
    return jnp.dot(a, b, preferred_element_type=F32)


def _split3(x):
    hi = x.astype(BF16)
    r = x - hi.astype(F32)
    mid = r.astype(BF16)
    lo = (r - mid.astype(F32)).astype(BF16)
    return hi, mid, lo


def _dot_f32_lhs(x, m01):
    hi, mid, lo = _split3(x)
    return _dot(hi, m01) + _dot(mid, m01) + _dot(lo, m01)


def _dot_f32_lhs2(x, m01):
    hi = x.astype(BF16)
    lo = (x - hi.astype(F32)).astype(BF16)
    return _dot(hi, m01) + _dot(lo, m01)


def _dot_f32_rhs(m01, x):
    hi, mid, lo = _split3(x)
    return _dot(m01, hi) + _dot(m01, mid) + _dot(m01, lo)


def _log_sigmoid(x):
    return jnp.minimum(x, 0.0) - jnp.log(1.0 + jnp.exp(-jnp.abs(x)))


def _mod_spec(mod, tm, rows_per_seq):
    d = mod.shape[-1]
    if mod.shape[2] == 1:
        return pl.BlockSpec((None, N_MOD, 1, d), lambda i: ((i * tm) // rows_per_seq, 0, 0, 0))
    return pl.BlockSpec((None, N_MOD, tm, d), lambda i: (0, 0, i, 0))


def _ada_kernel(c_ref, w_ref, b_ref, o_ref):
    c = c_ref[...]
    s = c * jax.nn.sigmoid(c)
    o_ref[...] = _dot(s.astype(BF16), w_ref[...].astype(BF16)) + b_ref[...]


def _ada(c_all, w_ada, b_ada):
    depth, d, _ = w_ada.shape
    r = c_all.shape[0]
    b4 = b_ada.reshape(depth, N_MOD, 1, d)
    return pl.pallas_call(
        _ada_kernel,
        grid=(depth, N_MOD),
        in_specs=[pl.BlockSpec((r, d), lambda l, k: (0, 0)),
                  pl.BlockSpec((None, d, d), lambda l, k: (l, 0, k)),
                  pl.BlockSpec((None, None, 1, d), lambda l, k: (l, k, 0, 0))],
        out_specs=pl.BlockSpec((None, None, r, d), lambda l, k: (l, k, 0, 0)),
        out_shape=jax.ShapeDtypeStruct((depth, N_MOD, r, d), F32),
        compiler_params=_cparams(2),
    )(c_all, w_ada, b4)


def _win_layout_kernel(w_ref, wa_ref, wg_ref, *, src, dst, rope_src, rope_dst, gate_src):
    wa_ref[...] = jnp.zeros(wa_ref.shape, wa_ref.dtype)
    for (s0, s1), d0 in zip(src, dst):
        wa_ref[:, d0:d0 + (s1 - s0)] = w_ref[:, s0:s1].astype(wa_ref.dtype)
    half = (rope_src[1] - rope_src[0]) // 2
    wa_ref[:, rope_dst:rope_dst + half] = w_ref[:, rope_src[0] + half:rope_src[1]].astype(wa_ref.dtype)
    wa_ref[:, rope_dst + half:rope_dst + 2 * half] = w_ref[:, rope_src[0]:rope_src[0] + half].astype(wa_ref.dtype)
    wg_ref[...] = w_ref[:, gate_src:gate_src + wg_ref.shape[1]].astype(wg_ref.dtype)


def _win_layout(w_in_l, dims):
    d, dp, hd, q_lora, kv_lora, rope = dims
    widths = (dp, 3 * hd, N_HEADS, q_lora, kv_lora, rope, 3 * hd, N_BRANCH * d)
    offs = [0]
    for w in widths:
        offs.append(offs[-1] + w)
    pool, fox, f, cq, ckv, kr, sb, gate = [(offs[i], offs[i + 1]) for i in range(len(widths))]
    small = dp + 3 * hd + q_lora + kv_lora + 3 * hd
    src = (pool, fox, cq, ckv, sb, kr, f)
    dst = (0, dp, dp + 3 * hd, dp + 3 * hd + q_lora, dp + 3 * hd + q_lora + kv_lora, small, small + 2 * rope)
    rows = LANES
    return pl.pallas_call(
        functools.partial(_win_layout_kernel, src=src, dst=dst, rope_src=kr, rope_dst=small + rope, gate_src=gate[0]),
        grid=(d // rows,),
        in_specs=[pl.BlockSpec((rows, w_in_l.shape[1]), lambda i: (i, 0))],
        out_specs=[pl.BlockSpec((rows, small + LANES), lambda i: (i, 0)),
                   pl.BlockSpec((rows, N_BRANCH * d), lambda i: (i, 0))],
        out_shape=[jax.ShapeDtypeStruct((d, small + LANES), BF16), jax.ShapeDtypeStruct((d, N_BRANCH * d), BF16)],
        compiler_params=_cparams(1),
    )(w_in_l)


def _inproj_kernel(x_ref, mod_ref, w_ref, *out_refs, widths):
    h = x_ref[...] * (1.0 + mod_ref[1]) + mod_ref[0]
    u = _dot(h.astype(BF16), w_ref[...])
    off = 0
    for o_ref, w in zip(out_refs, widths):
        o_ref[...] = u[:, off:off + w]
        off += w


def _inproj(x2d, mod, w_a, widths, tm, rows_per_seq):
    n, d = x2d.shape
    return pl.pallas_call(
        functools.partial(_inproj_kernel, widths=widths),
        grid=(n // tm,),
        in_specs=[pl.BlockSpec((tm, d), lambda i: (i, 0)),
                  _mod_spec(mod, tm, rows_per_seq),
                  pl.BlockSpec(w_a.shape, lambda i: (0, 0))],
        out_specs=[pl.BlockSpec((tm, w), lambda i: (i, 0)) for w in widths],
        out_shape=[jax.ShapeDtypeStruct((n, w), F32) for w in widths],
        compiler_params=_cparams(1),
    )(x2d, mod, w_a)


def _pool_kernel(u_ref, pre_ref, w_ref, sc_ref, o_ref, new_ref, z_ref, *, t, pos0):
    total, dp = z_ref.shape
    base = POOL_BUF + 1
    z_ref[0:1, :] = jnp.zeros((1, dp), F32)
    z_ref[1:base, :] = pre_ref[...]
    z_ref[base:base + t, :] = u_ref[...]
    if total > base + t:
        z_ref[base + t:total, :] = jnp.zeros((total - base - t, dp), F32)
    z = z_ref[...]
    sums = []
    s = z
    for step in (1, 2, 4, 8):
        s = s + pltpu.roll(s, step, 0)
        sums.append(s)
    lane = lax.broadcasted_iota(jnp.int32, (t, dp), 1)
    group = dp // len(POOL_WINDOWS)
    win = sums[-1][base:base + t]
    wlen = jnp.full((t, dp), float(POOL_WINDOWS[-1]), F32)
    for g in range(len(POOL_WINDOWS) - 2, -1, -1):
        sel = lane < (g + 1) * group
        win = jnp.where(sel, sums[g][base:base + t], win)
        wlen = jnp.where(sel, float(POOL_WINDOWS[g]), wlen)
    pos = (pos0 + lax.broadcasted_iota(jnp.int32, (t, dp), 0)).astype(F32)
    cnt = jnp.minimum(wlen, pos + 1.0)
    pooled = win / cnt - u_ref[...]
    o_ref[...] = (_dot(pooled.astype(BF16), w_ref[...]) * sc_ref[...]).astype(o_ref.dtype)
    new_ref[...] = z_ref[t + 1:t + 1 + POOL_BUF, :]


def _pool(u_pool, prefix, w_bd, scale, pos0):
    b, t, dp = u_pool.shape
    total = -(-(POOL_BUF + 1 + t) // SUBLANES) * SUBLANES
    return pl.pallas_call(
        functools.partial(_pool_kernel, t=t, pos0=pos0),
        grid=(b,),
        in_specs=[pl.BlockSpec((None, t, dp), lambda i: (i, 0, 0)),
                  pl.BlockSpec((None, POOL_BUF, dp), lambda i: (i, 0, 0)),
                  pl.BlockSpec((dp, dp), lambda i: (0, 0)),
                  pl.BlockSpec((1, dp), lambda i: (0, 0))],
        out_specs=[pl.BlockSpec((None, t, dp), lambda i: (i, 0, 0)),
                   pl.BlockSpec((None, POOL_BUF, dp), lambda i: (i, 0, 0))],
        out_shape=[jax.ShapeDtypeStruct((b, t, dp), BF16),
                   jax.ShapeDtypeStruct((b, POOL_BUF, dp), F32)],
        scratch_shapes=[pltpu.VMEM((total, dp), F32)],
        compiler_params=_cparams(1),
    )(u_pool, prefix, w_bd, scale)


def _fox_prep_kernel(s_ref, b_ref, lf_ref, c_ref, *, t, col):
    x = s_ref[...] + b_ref[...]
    lf = _log_sigmoid(x)
    lf_ref[...] = lf[:, col:col + N_HEADS]
    if t % LANES == 0:
        r = lax.broadcasted_iota(jnp.int32, (LANES, LANES), 0)
        c = lax.broadcasted_iota(jnp.int32, (LANES, LANES), 1)
        tri = jnp.where(r >= c, 1.0, 0.0).astype(BF16)
        carry = jnp.zeros((1, LANES), F32)
        for ch in range(t // LANES):
            cs = _dot_f32_rhs(tri, lf[ch * LANES:(ch + 1) * LANES]) + carry
            c_ref[ch * LANES:(ch + 1) * LANES, :] = cs[:, col:col + N_HEADS]
            carry = cs[LANES - 1:LANES, :]
    else:
        acc = jnp.zeros((1, LANES), F32)
        for i in range(t):
            acc = acc + lf[i:i + 1]
            c_ref[i:i + 1, :] = acc[:, col:col + N_HEADS]


def _fox_prep(small, bias_row, col):
    b, t, w = small.shape
    return pl.pallas_call(
        functools.partial(_fox_prep_kernel, t=t, col=col),
        grid=(b,),
        in_specs=[pl.BlockSpec((None, t, w), lambda i: (i, 0, 0)),
                  pl.BlockSpec((1, w), lambda i: (0, 0))],
        out_specs=[pl.BlockSpec((None, t, N_HEADS), lambda i: (i, 0, 0))] * 2,
        out_shape=[jax.ShapeDtypeStruct((b, t, N_HEADS), F32)] * 2,
        compiler_params=_cparams(1),
    )(small, bias_row)


def _softmax_step(carry, s, v):
    m, l, acc = carry
    m_new = jnp.maximum(m, jnp.max(s, axis=-1, keepdims=True))
    p = jnp.exp(s - m_new)
    alpha = jnp.exp(m - m_new)
    l = alpha * l + jnp.sum(p, axis=-1, keepdims=True)
    acc = alpha * acc + _dot(p.astype(BF16), v)
    return m_new, l, acc


def _causal_mask(tq, tk, strict):
    r = lax.broadcasted_iota(jnp.int32, (tq, tk), 0)
    c = lax.broadcasted_iota(jnp.int32, (tq, tk), 1)
    return (c < r) if strict else (c <= r)


def _fox_attn_kernel(q_ref, k_ref, v_ref, ct_ref, cs_ref, o_ref, *, tile, dh):
    i = pl.program_id(1)
    scale = dh ** -0.5
    mask = _causal_mask(tile, tile, False)
    heads = [slice(h * dh, (h + 1) * dh) for h in range(N_HEADS)]
    qs = [(q_ref[:, sl] * scale).astype(BF16) for sl in heads]
    cts = [ct_ref[:, h:h + 1] for h in range(N_HEADS)]

    def step(j, carry, masked):
        rows = pl.ds(pl.multiple_of(j * tile, tile), tile)
        out = []
        for h, sl in enumerate(heads):
            s = _nt(qs[h], k_ref[rows, sl].astype(BF16)) + (cts[h] - cs_ref[h, j])
            if masked:
                s = jnp.where(mask, s, NEG)
            out.append(_softmax_step(carry[h], s, v_ref[rows, sl].astype(BF16)))
        return tuple(out)

    init = tuple((jnp.full((tile, 1), NEG, F32), jnp.zeros((tile, 1), F32), jnp.zeros((tile, dh), F32))
                 for _ in heads)
    carry = lax.fori_loop(0, i, lambda j, c: step(j, c, False), init)
    carry = step(i, carry, True)
    for sl, (m, l, acc) in zip(heads, carry):
        o_ref[:, sl] = (acc / l).astype(o_ref.dtype)


def _fox_attn(u_fox, c_t, c_s, tile):
    b, t, w = u_fox.shape
    hd = w // 3
    dh = hd // N_HEADS
    nq = t // tile
    return pl.pallas_call(
        functools.partial(_fox_attn_kernel, tile=tile, dh=dh),
        grid=(b, nq),
        in_specs=[pl.BlockSpec((None, tile, hd), lambda bi, i: (bi, i, 0)),
                  pl.BlockSpec((None, t, hd), lambda bi, i: (bi, 0, 1)),
                  pl.BlockSpec((None, t, hd), lambda bi, i: (bi, 0, 2)),
                  pl.BlockSpec((None, tile, N_HEADS), lambda bi, i: (bi, i, 0)),
                  pl.BlockSpec((None, N_HEADS, nq, 1, tile), lambda bi, i: (bi, 0, 0, 0, 0))],
        out_specs=pl.BlockSpec((None, tile, hd), lambda bi, i: (bi, i, 0)),
        out_shape=jax.ShapeDtypeStruct((b, t, hd), BF16),
        compiler_params=_cparams(2),
    )(u_fox, u_fox, u_fox, c_t, c_s)


def _mla_attn_kernel(q_ref, lat_ref, wuv_ref, o_ref, *, tile, lora, scale):
    i = pl.program_id(1)
    wq = q_ref.shape[1] // N_HEADS
    q = jnp.concatenate([q_ref[:, h * wq:(h + 1) * wq] for h in range(N_HEADS)], axis=0)
    rows_all = N_HEADS * tile
    r = lax.broadcasted_iota(jnp.int32, (rows_all, tile), 0) % tile
    c = lax.broadcasted_iota(jnp.int32, (rows_all, tile), 1)
    mask = c <= r

    def step(j, carry, masked):
        lat = lat_ref[pl.ds(pl.multiple_of(j * tile, tile), tile), :]
        s = _nt(q, lat) * scale
        if masked:
            s = jnp.where(mask, s, NEG)
        return _softmax_step(carry, s, lat[:, :lora])

    init = (jnp.full((rows_all, 1), NEG, F32), jnp.zeros((rows_all, 1), F32), jnp.zeros((rows_all, lora), F32))
    carry = lax.fori_loop(0, i, lambda j, c: step(j, c, False), init)
    m, l, acc = step(i, carry, True)
    o = (acc / l).astype(BF16)
    o = jnp.concatenate([o[h * tile:(h + 1) * tile] for h in range(N_HEADS)], axis=-1)
    o_ref[...] = _dot(o, wuv_ref[...]).astype(o_ref.dtype)


def _mla_attn(q_cat, lat_pad, w_uv_bd, tile, lora, scale):
    b, t, wq = q_cat.shape
    wl = lat_pad.shape[-1]
    wo = w_uv_bd.shape[1]
    return pl.pallas_call(
        functools.partial(_mla_attn_kernel, tile=tile, lora=lora, scale=scale),
        grid=(b, t // tile),
        in_specs=[pl.BlockSpec((None, tile, wq), lambda bi, i: (bi, i, 0)),
                  pl.BlockSpec((None, t, wl), lambda bi, i: (bi, 0, 0)),
                  pl.BlockSpec(w_uv_bd.shape, lambda bi, i: (0, 0))],
        out_specs=pl.BlockSpec((None, tile, wo), lambda bi, i: (bi, i, 0)),
        out_shape=jax.ShapeDtypeStruct((b, t, wo), BF16),
        compiler_params=_cparams(2),
    )(q_cat, lat_pad, w_uv_bd)


def _suffix_matrix(n, with_total):
    j = lax.broadcasted_iota(jnp.int32, (n, n), 0)
    s = lax.broadcasted_iota(jnp.int32, (n, n), 1)
    m = jnp.where(j > s, 1.0, 0.0).astype(BF16)
    if with_total:
        m = jnp.concatenate([m, jnp.ones((n, n), BF16)], axis=1)
    return m


def _row_sum(x):
    return jnp.sum(x, axis=-1, keepdims=True)


def _sb_weights(z, carry_a, msuf, valid):
    ls = _log_sigmoid(z)
    lfail = ls - z
    if valid is not None:
        lfail = jnp.where(valid, lfail, 0.0)
    w = jnp.exp(ls + _dot_f32_lhs2(lfail, msuf) + carry_a)
    if valid is not None:
        w = jnp.where(valid, w, 0.0)
    return w, carry_a + _row_sum(lfail)


def _sb_block(z, v, carry_a, acc, msuf, valid):
    n = z.shape[1]
    ls = _log_sigmoid(z)
    lfail = ls - z
    if valid is not None:
        lfail = jnp.where(valid, lfail, 0.0)
    ct = _dot_f32_lhs2(lfail, msuf)
    w = jnp.exp(ls + ct[:, :n] + carry_a)
    if valid is not None:
        w = jnp.where(valid, w, 0.0)
    return carry_a + ct[:, n:], acc + _dot(w.astype(BF16), v)


def _sb_attn_kernel(q_ref, k_ref, v_ref, o_ref, *, tile, dh):
    i = pl.program_id(1)
    scale = dh ** -0.5
    valid = _causal_mask(tile, tile, True)
    msuf = _suffix_matrix(tile, True)
    heads = [slice(h * dh, (h + 1) * dh) for h in range(N_HEADS)]
    qs = [(q_ref[:, sl] * scale).astype(BF16) for sl in heads]

    def step(j, carry, mask):
        rows = pl.ds(pl.multiple_of(j * tile, tile), tile)
        return tuple(_sb_block(_nt(qs[h], k_ref[rows, sl].astype(BF16)), v_ref[rows, sl].astype(BF16),
                               carry[h][0], carry[h][1], msuf, mask) for h, sl in enumerate(heads))

    init = tuple((jnp.zeros((tile, tile), F32), jnp.zeros((tile, dh), F32)) for _ in heads)
    carry = step(i, init, valid)
    carry = lax.fori_loop(0, i, lambda t, c: step(i - 1 - t, c, None), carry)
    for sl, (a, acc) in zip(heads, carry):
        o_ref[:, sl] = acc.astype(o_ref.dtype)


def _sb_attn(u_sb, tile):
    b, t, w = u_sb.shape
    hd = w // 3
    dh = hd // N_HEADS
    return pl.pallas_call(
        functools.partial(_sb_attn_kernel, tile=tile, dh=dh),
        grid=(b, t // tile),
        in_specs=[pl.BlockSpec((None, tile, hd), lambda bi, i: (bi, i, 0)),
                  pl.BlockSpec((None, t, hd), lambda bi, i: (bi, 0, 1)),
                  pl.BlockSpec((None, t, hd), lambda bi, i: (bi, 0, 2))],
        out_specs=pl.BlockSpec((None, tile, hd), lambda bi, i: (bi, i, 0)),
        out_shape=jax.ShapeDtypeStruct((b, t, hd), BF16),
        compiler_params=_cparams(2),
    )(u_sb, u_sb, u_sb)


def _rms(x, g):
    return x * lax.rsqrt(jnp.mean(jnp.square(x), axis=-1, keepdims=True) + RMS_EPS) * g


def _mla_prep_kernel(u_ref, s_ref, cos_ref, sin_ref, gq_ref, gkv_ref, wn_ref, wr_ref, wrs_ref, wuk_ref,
                     qcat_ref, latpad_ref, lat_ref, *, q_lora, kv_lora, rope):
    cos = cos_ref[...]
    sin = sin_ref[...]
    cq = _rms(u_ref[:, :q_lora], gq_ref[...]).astype(BF16)
    q_nope = _dot(cq, wn_ref[...])
    q_rope = _dot(cq, wr_ref[...])
    q_rope_sw = _dot(cq, wrs_ref[...])
    q_abs = _dot(q_nope.astype(BF16), wuk_ref[...])
    pieces = []
    for h in range(N_HEADS):
        blk = slice(h * LANES, (h + 1) * LANES)
        pieces.append(q_abs[:, h * kv_lora:(h + 1) * kv_lora])
        pieces.append(q_rope[:, blk] * cos + q_rope_sw[:, blk] * sin)
    qcat_ref[...] = jnp.concatenate(pieces, axis=-1).astype(qcat_ref.dtype)
    ckv = _rms(u_ref[:, q_lora:q_lora + kv_lora], gkv_ref[...])
    small = s_ref[...]
    kr = small * cos + pltpu.roll(small, LANES - rope, 1) * sin
    lane = lax.broadcasted_iota(jnp.int32, kr.shape, 1)
    kr = jnp.where(lane < rope, kr, 0.0)
    latpad_ref[...] = jnp.concatenate([ckv, kr], axis=-1).astype(latpad_ref.dtype)
    lat_ref[...] = jnp.concatenate([ckv, kr[:, :rope]], axis=-1)


def _mla_prep(u_mla, small, cos, sin, g_q, g_kv, w_nope, w_rope, w_rope_sw, w_uk_bd, tm, rope):
    n, wm = u_mla.shape
    q_lora = g_q.shape[1]
    kv_lora = g_kv.shape[1]
    ncs = cos.shape[0] // tm
    wq = N_HEADS * (kv_lora + LANES)
    full = lambda a: pl.BlockSpec(a.shape, lambda i: (0, 0))
    return pl.pallas_call(
        functools.partial(_mla_prep_kernel, q_lora=q_lora, kv_lora=kv_lora, rope=rope),
        grid=(n // tm,),
        in_specs=[pl.BlockSpec((tm, wm), lambda i: (i, 0)),
                  pl.BlockSpec((tm, LANES), lambda i: (i, 0)),
                  pl.BlockSpec((tm, LANES), lambda i: (i % ncs, 0)),
                  pl.BlockSpec((tm, LANES), lambda i: (i % ncs, 0)),
                  full(g_q), full(g_kv), full(w_nope), full(w_rope), full(w_rope_sw), full(w_uk_bd)],
        out_specs=[pl.BlockSpec((tm, wq), lambda i: (i, 0)),
                   pl.BlockSpec((tm, kv_lora + LANES), lambda i: (i, 0)),
                   pl.BlockSpec((tm, kv_lora + rope), lambda i: (i, 0))],
        out_shape=[jax.ShapeDtypeStruct((n, wq), BF16),
                   jax.ShapeDtypeStruct((n, kv_lora + LANES), BF16),
                   jax.ShapeDtypeStruct((n, kv_lora + rope), F32)],
        compiler_params=_cparams(1),
    )(u_mla, small, cos, sin, g_q, g_kv, w_nope, w_rope, w_rope_sw, w_uk_bd)


def _page_specs(block, layer, n_pages, pps):
    zeros = (0,) * len(block)

    def spec(r):
        return pl.BlockSpec((None, None) + block,
                            lambda b, c, pt, r=r: (layer, pt[b * n_pages + (n_pages - 1 - (c * pps + r))]) + zeros)
    return [spec(r) for r in range(pps)]


def _new_token_mask(rows, n_new, strict):
    r = lax.broadcasted_iota(jnp.int32, (rows, LANES), 0) % (rows // N_HEADS)
    c = lax.broadcasted_iota(jnp.int32, (rows, LANES), 1)
    limit = jnp.minimum(r if strict else r + 1, n_new)
    return c < limit


def _head_scores(q, kv_refs):
    qr = q.shape[0] // N_HEADS
    ss, vs = [], []
    for h in range(N_HEADS):
        kt = jnp.concatenate([ref[0, h].astype(BF16) for ref in kv_refs], axis=1)
        ss.append(_dot(q[h * qr:(h + 1) * qr], kt))
        vs.append(jnp.concatenate([ref[1, h].astype(BF16) for ref in kv_refs], axis=1))
    return jnp.concatenate(ss, axis=0), vs


def _head_pv(p, vs):
    qr = p.shape[0] // N_HEADS
    return jnp.concatenate([_nt(p[h * qr:(h + 1) * qr].astype(BF16), vs[h]) for h in range(N_HEADS)], axis=0)


def _stage_new_rows(pad_ref, new_ref):
    pad_ref[...] = jnp.zeros(pad_ref.shape, F32)
    pad_ref[0:new_ref.shape[0], :] = new_ref[...]


def _head_scores_rows(q, pad_ref):
    qr = q.shape[0] // N_HEADS
    hd = pad_ref.shape[1] // 2
    dh = hd // N_HEADS
    ss = [_nt(q[h * qr:(h + 1) * qr], pad_ref[:, h * dh:(h + 1) * dh].astype(BF16)) for h in range(N_HEADS)]
    vs = [pad_ref[:, hd + h * dh:hd + (h + 1) * dh].astype(BF16) for h in range(N_HEADS)]
    return jnp.concatenate(ss, axis=0), vs


def _head_pv_rows(p, vs):
    qr = p.shape[0] // N_HEADS
    return jnp.concatenate([_dot(p[h * qr:(h + 1) * qr].astype(BF16), vs[h]) for h in range(N_HEADS)], axis=0)


def _fox_sfx_kernel(x_ref, m_ref, o_ref):
    o_ref[...] = _dot_f32_lhs(x_ref[...], m_ref[...])


def _fox_sfx(logf_rows):
    n, page = logf_rows.shape
    tr = 2048 if n % 2048 == 0 else n
    m = _suffix_matrix(page, True)
    return pl.pallas_call(
        _fox_sfx_kernel,
        grid=(n // tr,),
        in_specs=[pl.BlockSpec((tr, page), lambda i: (i, 0)), pl.BlockSpec(m.shape, lambda i: (0, 0))],
        out_specs=pl.BlockSpec((tr, 2 * page), lambda i: (i, 0)),
        out_shape=jax.ShapeDtypeStruct((n, 2 * page), F32),
        compiler_params=_cparams(1),
    )(logf_rows, m)


def _softmax_update(m_ref, l_ref, acc_ref, s, pv, first):
    smax = jnp.max(s, axis=-1, keepdims=True)
    if first:
        m_new = smax
    else:
        m_old = m_ref[...]
        m_new = jnp.maximum(m_old, smax)
        alpha = jnp.exp(m_old - m_new)
    p = jnp.exp(s - m_new)
    l = jnp.sum(p, axis=-1, keepdims=True)
    acc = pv(p)
    if not first:
        l = l + alpha * l_ref[...]
        acc = acc + alpha * acc_ref[...]
    m_ref[...] = m_new
    l_ref[...] = l
    acc_ref[...] = acc
    return l, acc


def _fox_dec_kernel(pt_ref, q_ref, kvn_ref, gn_ref, *rest, pps, n_new):
    kv_refs, sfx_refs = rest[:pps], rest[pps:2 * pps]
    o_ref, m_ref, l_ref, acc_ref, car_ref, pad_ref = rest[2 * pps:]
    c = pl.program_id(1)
    q = q_ref[...]
    rows = q.shape[0]
    qr = rows // N_HEADS

    @pl.when(c == 0)
    def _():
        _stage_new_rows(pad_ref, kvn_ref)
        s, vs = _head_scores_rows(q, pad_ref)
        s = jnp.where(_new_token_mask(rows, n_new, False), s + gn_ref[...], NEG)
        _softmax_update(m_ref, l_ref, acc_ref, s, lambda p: _head_pv_rows(p, vs), True)
        car_ref[...] = jnp.zeros(car_ref.shape, F32)

    car = car_ref[0:N_HEADS, :]
    biases = []
    for r in range(pps):
        blk = sfx_refs[r][...]
        biases.append(blk[:, :LANES] + car)
        car = car + blk[:, LANES:]
    car_ref[0:N_HEADS, :] = car
    bias = jnp.concatenate(biases, axis=1)
    s, vs = _head_scores(q, kv_refs)
    s = s + jnp.concatenate([jnp.broadcast_to(bias[h:h + 1, :], (qr, bias.shape[1])) for h in range(N_HEADS)], axis=0)
    l, acc = _softmax_update(m_ref, l_ref, acc_ref, s, lambda p: _head_pv(p, vs), False)

    @pl.when(c == pl.num_programs(1) - 1)
    def _():
        o_ref[...] = acc / l


def _fox_dec(layer, q_rows, kv_new, g_new, cache_kv, sfx, page_table, n_new):
    db, rows, dh = q_rows.shape
    n_pages = page_table.shape[1]
    pps = min(PAGES_PER_STEP, n_pages)
    kv_block = cache_kv.shape[2:]
    sfx_block = sfx.shape[2:]
    grid_spec = pltpu.PrefetchScalarGridSpec(
        num_scalar_prefetch=1,
        grid=(db, n_pages // pps),
        in_specs=[pl.BlockSpec((None, rows, dh), lambda b, c, pt: (b, 0, 0)),
                  pl.BlockSpec((None,) + kv_new.shape[1:], lambda b, c, pt: (b, 0, 0)),
                  pl.BlockSpec((None, rows, LANES), lambda b, c, pt: (b, 0, 0))]
        + _page_specs(kv_block, layer, n_pages, pps) + _page_specs(sfx_block, layer, n_pages, pps),
        out_specs=pl.BlockSpec((None, rows, dh), lambda b, c, pt: (b, 0, 0)),
        scratch_shapes=[pltpu.VMEM((rows, 1), F32), pltpu.VMEM((rows, 1), F32), pltpu.VMEM((rows, dh), F32),
                        pltpu.VMEM((SUBLANES, LANES), F32), pltpu.VMEM((kv_block[-1], kv_new.shape[2]), F32)])
    return pl.pallas_call(
        functools.partial(_fox_dec_kernel, pps=pps, n_new=n_new),
        grid_spec=grid_spec,
        out_shape=jax.ShapeDtypeStruct((db, rows, dh), F32),
        compiler_params=_cparams(2),
    )(page_table.reshape(-1), q_rows, kv_new, g_new, *([cache_kv] * pps), *([sfx] * pps))


def _mla_dec_kernel(pt_ref, q_ref, latn_ref, wuv_ref, *rest, pps, n_new, lora, scale):
    lat_refs = rest[:pps]
    o_ref, m_ref, l_ref, acc_ref, pad_ref = rest[pps:]
    c = pl.program_id(1)
    q = q_ref[...]
    rows = q.shape[0]

    @pl.when(c == 0)
    def _():
        _stage_new_rows(pad_ref, latn_ref)
        lat = pad_ref[...].astype(BF16)
        s = jnp.where(_new_token_mask(rows, n_new, False), _nt(q, lat) * scale, NEG)
        _softmax_update(m_ref, l_ref, acc_ref, s, lambda p: _dot(p.astype(BF16), lat[:, :lora]), True)

    lat = jnp.concatenate([lat_refs[r][...].astype(BF16) for r in range(pps)], axis=1)
    l, acc = _softmax_update(m_ref, l_ref, acc_ref, _dot(q, lat) * scale,
                             lambda p: _nt(p.astype(BF16), lat[:lora]), False)

    @pl.when(c == pl.num_programs(1) - 1)
    def _():
        o_ref[...] = _dot((acc / l).astype(BF16), wuv_ref[...])


def _mla_dec(layer, q_rows, lat_new, w_uv_all, cache_lat, page_table, n_new, lora, scale):
    db, rows, wq = q_rows.shape
    n_pages = page_table.shape[1]
    pps = min(PAGES_PER_STEP, n_pages)
    lat_block = cache_lat.shape[2:]
    wo = w_uv_all.shape[1]
    grid_spec = pltpu.PrefetchScalarGridSpec(
        num_scalar_prefetch=1,
        grid=(db, n_pages // pps),
        in_specs=[pl.BlockSpec((None, rows, wq), lambda b, c, pt: (b, 0, 0)),
                  pl.BlockSpec((None,) + lat_new.shape[1:], lambda b, c, pt: (b, 0, 0)),
                  pl.BlockSpec(w_uv_all.shape, lambda b, c, pt: (0, 0))]
        + _page_specs(lat_block, layer, n_pages, pps),
        out_specs=pl.BlockSpec((None, rows, wo), lambda b, c, pt: (b, 0, 0)),
        scratch_shapes=[pltpu.VMEM((rows, 1), F32), pltpu.VMEM((rows, 1), F32), pltpu.VMEM((rows, lora), F32),
                        pltpu.VMEM((lat_block[-1], lat_new.shape[2]), F32)])
    return pl.pallas_call(
        functools.partial(_mla_dec_kernel, pps=pps, n_new=n_new, lora=lora, scale=scale),
        grid_spec=grid_spec,
        out_shape=jax.ShapeDtypeStruct((db, rows, wo), F32),
        compiler_params=_cparams(2),
    )(page_table.reshape(-1), q_rows, lat_new, w_uv_all, *([cache_lat] * pps))


def _sb_dec_kernel(pt_ref, q_ref, kvn_ref, *rest, pps, n_new):
    kv_refs = rest[:pps]
    o_ref, a_ref, acc_ref, pad_ref = rest[pps:]
    c = pl.program_id(1)
    q = q_ref[...]
    rows = q.shape[0]
    msuf = _suffix_matrix(LANES, False)

    @pl.when(c == 0)
    def _():
        _stage_new_rows(pad_ref, kvn_ref)
        z, vs = _head_scores_rows(q, pad_ref)
        w, a = _sb_weights(z, jnp.zeros(a_ref.shape, F32), msuf, _new_token_mask(rows, n_new, True))
        a_ref[...] = a
        acc_ref[...] = _head_pv_rows(w, vs)

    z, vs = _head_scores(q, kv_refs)
    ls = _log_sigmoid(z)
    lfail = ls - z
    pages = [lfail[:, r * LANES:(r + 1) * LANES] for r in range(pps)]
    ct = _dot_f32_lhs2(jnp.concatenate(pages, axis=0), msuf)
    a = a_ref[...]
    cums = []
    for r in range(pps):
        cums.append(ct[r * rows:(r + 1) * rows] + a)
        a = a + _row_sum(pages[r])
    a_ref[...] = a
    acc = acc_ref[...] + _head_pv(jnp.exp(ls + jnp.concatenate(cums, axis=1)), vs)
    acc_ref[...] = acc

    @pl.when(c == pl.num_programs(1) - 1)
    def _():
        o_ref[...] = acc


def _sb_dec(layer, q_rows, kv_new, cache_kv, page_table, n_new):
    db, rows, dh = q_rows.shape
    n_pages = page_table.shape[1]
    pps = min(PAGES_PER_STEP, n_pages)
    kv_block = cache_kv.shape[2:]
    grid_spec = pltpu.PrefetchScalarGridSpec(
        num_scalar_prefetch=1,
        grid=(db, n_pages // pps),
        in_specs=[pl.BlockSpec((None, rows, dh), lambda b, c, pt: (b, 0, 0)),
                  pl.BlockSpec((None,) + kv_new.shape[1:], lambda b, c, pt: (b, 0, 0))]
        + _page_specs(kv_block, layer, n_pages, pps),
        out_specs=pl.BlockSpec((None, rows, dh), lambda b, c, pt: (b, 0, 0)),
        scratch_shapes=[pltpu.VMEM((rows, 1), F32), pltpu.VMEM((rows, dh), F32),
                        pltpu.VMEM((kv_block[-1], kv_new.shape[2]), F32)])
    return pl.pallas_call(
        functools.partial(_sb_dec_kernel, pps=pps, n_new=n_new),
        grid_spec=grid_spec,
        out_shape=jax.ShapeDtypeStruct((db, rows, dh), F32),
        compiler_params=_cparams(2),
    )(page_table.reshape(-1), q_rows, kv_new, *([cache_kv] * pps))


def _layer_norm(v, g, b):
    mu = jnp.mean(v, axis=-1, keepdims=True)
    var = jnp.mean(jnp.square(v - mu), axis=-1, keepdims=True)
    return (v - mu) * lax.rsqrt(var + LN_EPS) * g + b


def _merge_kernel(x_ref, mod_ref, ba_ref, bb_ref, bc_ref, bd_ref, wg_ref, wb_ref, wo_ref, g_ref, b_ref,
                  wr_ref, br_ref, cnt0_ref, h2_all_ref, x1_ref, h2_ref, ti_ref, tp_ref, rk_ref, cnt_ref, *, alpha):
    del h2_all_ref
    @pl.when(pl.program_id(0) == 0)
    def _():
        cnt_ref[...] = cnt0_ref[...]

    x = x_ref[...]
    tm, d = x.shape
    h = (x * (1.0 + mod_ref[1]) + mod_ref[0]).astype(BF16)
    y = None
    for n, br in enumerate((ba_ref, bb_ref, bc_ref, bd_ref)):
        gate = jax.nn.sigmoid(_dot(h, wg_ref[:, n * d:(n + 1) * d]))
        term = gate * _dot(br[...], wb_ref[n])
        y = term if y is None else y + term
    y = _dot(y.astype(BF16), wo_ref[...])
    x1 = _layer_norm(alpha * x + (1.0 + mod_ref[2]) * y, g_ref[...], b_ref[...])
    x1_ref[...] = x1
    h2 = x1 * (1.0 + mod_ref[4]) + mod_ref[3]
    h2_ref[...] = h2
    logits = _dot(h2.astype(BF16), wr_ref[...]) + br_ref[...]
    ne = logits.shape[1]
    lane = lax.broadcasted_iota(jnp.int32, logits.shape, 1).astype(F32)
    kcol = lax.broadcasted_iota(jnp.int32, (tm, TOP_K), 1)
    tv = jnp.zeros((tm, TOP_K), F32)
    ti = jnp.zeros((tm, TOP_K), F32)
    picks = []
    for k in range(TOP_K):
        m = jnp.max(logits, axis=-1, keepdims=True)
        idx = jnp.min(jnp.where(logits == m, lane, float(ne)), axis=-1, keepdims=True)
        pick = lane == idx
        picks.append(pick)
        tv = jnp.where(kcol == k, m, tv)
        ti = jnp.where(kcol == k, idx, ti)
        logits = jnp.where(pick, -jnp.inf, logits)
    e = jnp.exp(tv - tv[:, 0:1])
    ti_ref[...] = ti.astype(jnp.int32)
    tp_ref[...] = e / jnp.sum(e, axis=-1, keepdims=True)
    chosen = None
    for pick in picks:
        one = jnp.where(pick, 1.0, 0.0)
        chosen = one if chosen is None else chosen + one
    r = lax.broadcasted_iota(jnp.int32, (tm, tm), 0)
    c = lax.broadcasted_iota(jnp.int32, (tm, tm), 1)
    earlier = jnp.where(c < r, 1.0, 0.0).astype(BF16)
    base = cnt_ref[...] + _dot(earlier, chosen.astype(BF16))
    rk = jnp.zeros((tm, TOP_K), F32)
    for k, pick in enumerate(picks):
        rk = jnp.where(kcol == k, jnp.sum(jnp.where(pick, base, 0.0), axis=-1, keepdims=True), rk)
    rk_ref[...] = rk.astype(jnp.int32)
    cnt_ref[...] = cnt_ref[...] + jnp.sum(chosen, axis=0, keepdims=True)


def _merge(x2d, mod, branches, w_gate, w_branch, w_out, ln_g, ln_b, w_router, b_router, counts, h2_all, row0,
           tm, rows_per_seq, alpha):
    n, d = x2d.shape
    full = lambda a: pl.BlockSpec(a.shape, lambda i: (0,) * a.ndim)
    row = lambda w: pl.BlockSpec((tm, w), lambda i: (i, 0))
    n_in = 2 + len(branches) + 8
    return pl.pallas_call(
        functools.partial(_merge_kernel, alpha=alpha),
        grid=(n // tm,),
        in_specs=[row(d), _mod_spec(mod, tm, rows_per_seq)] + [row(b.shape[1]) for b in branches]
        + [full(w_gate), full(w_branch), full(w_out), full(ln_g), full(ln_b), full(w_router), full(b_router),
           full(counts), pl.BlockSpec(memory_space=pl.ANY)],
        out_specs=[row(d), pl.BlockSpec((tm, d), lambda i: (row0 // tm + i, 0)), row(TOP_K), row(TOP_K), row(TOP_K),
                   full(counts)],
        input_output_aliases={n_in: 1},
        out_shape=[jax.ShapeDtypeStruct((n, d), F32), jax.ShapeDtypeStruct(h2_all.shape, F32),
                   jax.ShapeDtypeStruct((n, TOP_K), jnp.int32), jax.ShapeDtypeStruct((n, TOP_K), F32),
                   jax.ShapeDtypeStruct((n, TOP_K), jnp.int32), jax.ShapeDtypeStruct(counts.shape, F32)],
        compiler_params=_cparams(1),
    )(x2d, mod, *branches, w_gate, w_branch, w_out, ln_g, ln_b, w_router, b_router, counts, h2_all)


def _expert_kernel(be_ref, nb_ref, x_ref, w1_ref, b1g_ref, b1l_ref, w2_ref, b2_ref, o_ref,
                   wt_ref, wg_ref, wl_ref, w2b_ref):
    i = pl.program_id(0)
    f = wg_ref.shape[0]

    @pl.when((i == 0) | (be_ref[i] != be_ref[jnp.maximum(i - 1, 0)]))
    def _():
        for c in range(wt_ref.shape[0]):
            cols = slice(c * LANES, (c + 1) * LANES)
            wt_ref[c] = w1_ref[cols, :].T
            wg_ref[:, cols] = wt_ref[c, pl.ds(0, f, stride=2), :].astype(BF16)
            wl_ref[:, cols] = wt_ref[c, pl.ds(1, f, stride=2), :].astype(BF16)
        w2b_ref[...] = w2_ref[...].astype(BF16)

    @pl.when(i < nb_ref[0])
    def _():
        x = x_ref[...].astype(BF16)
        glu = jnp.minimum(_nt(x, wg_ref[...]) + b1g_ref[...], SWIGLU_LIMIT)
        lin = jnp.clip(_nt(x, wl_ref[...]) + b1l_ref[...], -SWIGLU_LIMIT, SWIGLU_LIMIT)
        act = glu * jax.nn.sigmoid(SWIGLU_ALPHA * glu) * (lin + 1.0)
        o_ref[...] = _dot(act.astype(BF16), w2b_ref[...]) + b2_ref[...]


def _experts(layer, block_exp, n_used, xs, w1, b1g, b1l, w2, b2):
    n_slots, d = xs.shape
    f2 = w1.shape[-1]
    f = f2 // 2
    nb = n_slots // MOE_ROWS
    blk = lambda i, be, nu: (jnp.minimum(i, nu[0] - 1), 0)
    expert = lambda i, be, nu: be[jnp.minimum(i, nu[0] - 1)]
    wspec = lambda s: pl.BlockSpec((None, None) + s, lambda i, be, nu: (layer, expert(i, be, nu), 0, 0))
    bspec = lambda w: pl.BlockSpec((None, 1, w), lambda i, be, nu: (expert(i, be, nu), 0, 0))
    grid_spec = pltpu.PrefetchScalarGridSpec(
        num_scalar_prefetch=2,
        grid=(nb,),
        in_specs=[pl.BlockSpec((MOE_ROWS, d), blk), wspec((d, f2)), bspec(f), bspec(f), wspec((f, d)), bspec(d)],
        out_specs=pl.BlockSpec((MOE_ROWS, d), blk),
        scratch_shapes=[pltpu.VMEM((d // LANES, f2, LANES), F32), pltpu.VMEM((f, d), BF16), pltpu.VMEM((f, d), BF16),
                        pltpu.VMEM((f, d), BF16)])
    return pl.pallas_call(
        _expert_kernel,
        grid_spec=grid_spec,
        out_shape=jax.ShapeDtypeStruct((n_slots, d), F32),
        compiler_params=_cparams(1),
    )(block_exp, n_used, xs, w1, b1g, b1l, w2, b2)


def _final_kernel(x_ref, mod_ref, *rest, alpha):
    y_refs, (p_ref, g_ref, b_ref, o_ref) = rest[:TOP_K], rest[TOP_K:]
    p = p_ref[...]
    f = None
    for k in range(TOP_K):
        term = p[:, k:k + 1] * y_refs[k][...]
        f = term if f is None else f + term
    o_ref[...] = _layer_norm(alpha * x_ref[...] + (1.0 + mod_ref[5]) * f, g_ref[...], b_ref[...])


def _final(x1, mod, ys, probs, ln_g, ln_b, tm, rows_per_seq, alpha):
    n, d = x1.shape
    full = lambda a: pl.BlockSpec(a.shape, lambda i: (0,) * a.ndim)
    return pl.pallas_call(
        functools.partial(_final_kernel, alpha=alpha),
        grid=(n // tm,),
        in_specs=[pl.BlockSpec((tm, d), lambda i: (i, 0)), _mod_spec(mod, tm, rows_per_seq)]
        + [pl.BlockSpec((tm, d), lambda i: (i, 0))] * TOP_K
        + [pl.BlockSpec((tm, TOP_K), lambda i: (i, 0)), full(ln_g), full(ln_b)],
        out_specs=pl.BlockSpec((tm, d), lambda i: (i, 0)),
        out_shape=jax.ShapeDtypeStruct((n, d), F32),
        compiler_params=_cparams(1),
    )(x1, mod, *ys, probs, ln_g, ln_b)


def _rope_tables(pos, rope):
    half = rope // 2
    inv = ROPE_BASE ** (-jnp.arange(half, dtype=F32) / half)
    ang = pos.astype(F32)[:, None] * inv[None, :]
    cos, sin = jnp.cos(ang), jnp.sin(ang)
    reps = LANES // rope
    return (jnp.tile(jnp.concatenate([cos, cos], -1), (1, reps)),
            jnp.tile(jnp.concatenate([-sin, sin], -1), (1, reps)))


def _swap_halves(w):
    half = w.shape[-1] // 2
    return jnp.concatenate([w[..., half:], w[..., :half]], axis=-1)


def _layer_weights(l, p, dims):
    d, dp, hd, q_lora, kv_lora, rope = dims
    w_a, w_gate = _win_layout(p['w_in'][l], dims)
    eye = jnp.eye(N_HEADS, dtype=F32)
    w_uq = p['w_mla_uq'][l]
    nope = w_uq.shape[-1] - rope
    pad = lambda w: jnp.pad(w, ((0, 0), (0, 0), (0, LANES - rope))).reshape(q_lora, N_HEADS * LANES)
    w_rope = w_uq[:, :, nope:]
    lw = {
        'w_a': w_a,
        'w_gate': w_gate,
        'w_pool_bd': jnp.einsum('gcd,gh->gchd', p['w_pool_mix'][l], jnp.eye(len(POOL_WINDOWS), dtype=F32)
                                ).reshape(dp, dp).astype(BF16),
        'pool_scale': p['pool_scale'][l].reshape(1, dp),
        'fox_bias': jnp.zeros((1, LANES), F32).at[0, 2 * rope:2 * rope + N_HEADS].set(p['b_fox_forget'][l]),
        'g_q': p['mla_q_norm'][l].reshape(1, q_lora),
        'g_kv': p['mla_kv_norm'][l].reshape(1, kv_lora),
        'w_nope': w_uq[:, :, :nope].reshape(q_lora, N_HEADS * nope).astype(BF16),
        'w_rope': pad(w_rope).astype(BF16),
        'w_rope_sw': pad(_swap_halves(w_rope)).astype(BF16),
        'w_uk_bd': jnp.einsum('chn,hg->hngc', p['w_mla_uk'][l], eye).reshape(N_HEADS * nope, N_HEADS * kv_lora).astype(BF16),
        'w_uv_bd': jnp.einsum('chv,hg->hcgv', p['w_mla_uv'][l], eye).reshape(N_HEADS * kv_lora, -1).astype(BF16),
        'w_uv_all': p['w_mla_uv'][l].reshape(kv_lora, -1).astype(BF16),
        'w_branch': p['w_branch'][l].astype(BF16),
        'w_out': p['w_out'][l].astype(BF16),
        'ln1_g': p['ln1_g'][l].reshape(1, d), 'ln1_b': p['ln1_b'][l].reshape(1, d),
        'ln2_g': p['ln2_g'][l].reshape(1, d), 'ln2_b': p['ln2_b'][l].reshape(1, d),
        'b_router': p['b_router'][l].reshape(1, -1),
        'b1g': p['b_exp1'][l][:, None, 0::2], 'b1l': p['b_exp1'][l][:, None, 1::2],
        'b2': p['b_exp2'][l][:, None, :],
        'w_router': p['w_router'][l].astype(BF16),
    }
    lw['widths'] = (dp, 3 * hd, q_lora + kv_lora, 3 * hd, LANES)
    lw['nope'] = nope
    return lw


def _routing(top_i, rank, counts):
    n = top_i.shape[0]
    nk = n * TOP_K
    n_experts = counts.shape[-1]
    sizes = counts.reshape(n_experts).astype(jnp.int32)
    padded = (sizes + MOE_ROWS - 1) // MOE_ROWS * MOE_ROWS
    ends = jnp.cumsum(padded)
    starts = ends - padded
    first = jnp.cumsum(sizes) - sizes
    dest = starts[top_i] + rank
    n_blocks = -(-(nk + n_experts * (MOE_ROWS - 1)) // MOE_ROWS)
    blk_start = jnp.arange(n_blocks, dtype=jnp.int32) * MOE_ROWS
    block_exp = jnp.minimum(jnp.sum(ends[None, :] <= blk_start[:, None], axis=1), n_experts - 1).astype(jnp.int32)
    bits = max(nk - 1, 1).bit_length()
    keys = jnp.sort(top_i.reshape(nk) * (1 << bits) + jnp.arange(nk, dtype=jnp.int32))
    order = keys & ((1 << bits) - 1)
    r = (blk_start - starts[block_exp])[:, None] + jnp.arange(MOE_ROWS, dtype=jnp.int32)[None, :]
    pair = order[jnp.clip(first[block_exp][:, None] + r, 0, nk - 1).reshape(-1)]
    slot_tok = jnp.where((r < sizes[block_exp][:, None]).reshape(-1), pair // TOP_K, 0).astype(jnp.int32)
    n_used = (ends[-1] // MOE_ROWS).astype(jnp.int32).reshape(1)
    return slot_tok, dest, block_exp, n_used


def _head_rows(q, scale):
    db, dt, hd = q.shape
    dh = hd // N_HEADS
    q4 = jnp.transpose((q * scale).reshape(db, dt, N_HEADS, dh), (0, 2, 1, 3))
    q4 = jnp.pad(q4, ((0, 0), (0, 0), (0, DEC_ROWS - dt), (0, 0)))
    return q4.reshape(db, N_HEADS * DEC_ROWS, dh).astype(BF16)


def _head_out(o, dt):
    db, _, dv = o.shape
    o4 = jnp.transpose(o.reshape(db, N_HEADS, DEC_ROWS, dv)[:, :, :dt], (0, 2, 1, 3))
    return o4.reshape(db * dt, N_HEADS * dv).astype(BF16)


def _diag_heads(o, dt):
    db, _, hv = o.shape
    dv = hv // N_HEADS
    o5 = o.reshape(db, N_HEADS, dt, N_HEADS, dv)
    d = jnp.stack([o5[:, h, :, h, :] for h in range(N_HEADS)], axis=2)
    return d.reshape(db * dt, hv).astype(BF16)


def kernel(x_prompt, x_sample, c_prompt, c_sample, cache_fox_kv, cache_fox_logf, cache_mla, cache_sb_kv, state_pool, page_table, w_ada, b_ada, w_in, b_fox_forget, w_pool_mix, pool_scale, mla_q_norm, w_mla_uq, mla_kv_norm, w_mla_uk, w_mla_uv, w_branch, w_out, ln1_g, ln1_b, w_router, b_router, w_exp1, b_exp1, w_exp2, b_exp2, ln2_g, ln2_b):
    p = dict(w_in=w_in, b_fox_forget=b_fox_forget, w_pool_mix=w_pool_mix, pool_scale=pool_scale,
             mla_q_norm=mla_q_norm, w_mla_uq=w_mla_uq, mla_kv_norm=mla_kv_norm, w_mla_uk=w_mla_uk,
             w_mla_uv=w_mla_uv, w_branch=w_branch, w_out=w_out, ln1_g=ln1_g, ln1_b=ln1_b, w_router=w_router,
             b_router=b_router, w_exp1=w_exp1, b_exp1=b_exp1, w_exp2=w_exp2, b_exp2=b_exp2, ln2_g=ln2_g, ln2_b=ln2_b)
    b, t, d = x_prompt.shape
    db, dt, _ = x_sample.shape
    depth = w_ada.shape[0]
    n_pages = page_table.shape[1]
    page = cache_fox_kv.shape[2]
    past = n_pages * page
    dp = state_pool.shape[-1]
    hd = cache_fox_kv.shape[-1] * cache_fox_kv.shape[-2]
    dh = hd // N_HEADS
    q_lora = mla_q_norm.shape[1]
    kv_lora = mla_kv_norm.shape[1]
    rope = cache_mla.shape[-1] - kv_lora
    n_experts = w_router.shape[-1]
    alpha = (2 * depth) ** 0.25
    dims = (d, dp, hd, q_lora, kv_lora, rope)
    np_rows, ns_rows = b * t, db * dt
    tm = min(ROW_TILE, t)
    tile = min(ATT_TILE, t)
    sb_tile = min(SB_TILE, t)
    nq = t // tile

    mods = _ada(jnp.concatenate([c_prompt, c_sample], axis=0), w_ada, b_ada)
    cos_p, sin_p = _rope_tables(jnp.arange(t), rope)
    cos_s, sin_s = (jnp.tile(a, (db, 1)) for a in _rope_tables(past + jnp.arange(dt), rope))
    fox_kv_t = jnp.transpose(cache_fox_kv, (0, 1, 3, 4, 5, 2))
    sb_kv_t = jnp.transpose(cache_sb_kv, (0, 1, 3, 4, 5, 2))
    mla_t = jnp.transpose(cache_mla, (0, 1, 3, 2))
    logf_t = jnp.transpose(cache_fox_logf, (0, 1, 3, 2))
    sfx = _fox_sfx(logf_t.reshape(-1, page)).reshape(logf_t.shape[:3] + (2 * page,))
    zero_counts = jnp.zeros((1, n_experts), F32)
    h2_buf = jnp.zeros((np_rows + ns_rows, d), F32)

    xp = x_prompt.reshape(np_rows, d)
    xs = x_sample.reshape(ns_rows, d)
    outs = {k: [] for k in ('fox_p', 'fox_s', 'lf_p', 'lf_s', 'mla_p', 'mla_s', 'sb_p', 'sb_s', 'pool_p', 'pool_s')}
    for l in range(depth):
        lw = _layer_weights(l, p, dims)
        mod_p = jnp.transpose(mods[l, :, :b], (1, 0, 2))[:, :, None, :]
        mod_s = jnp.repeat(mods[l, :, b:], dt, axis=1)[None]
        mla_scale = (lw['nope'] + rope) ** -0.5

        u_pool, u_fox, u_mla, u_sb, u_small = _inproj(xp, mod_p, lw['w_a'], lw['widths'], tm, t)
        br_a, pool_new = _pool(u_pool.reshape(b, t, dp), jnp.zeros((b, POOL_BUF, dp), F32),
                               lw['w_pool_bd'], lw['pool_scale'], 0)
        lf, cum = _fox_prep(u_small.reshape(b, t, LANES), lw['fox_bias'], 2 * rope)
        c_s = jnp.transpose(cum, (0, 2, 1)).reshape(b, N_HEADS, nq, 1, tile)
        br_b = _fox_attn(u_fox.reshape(b, t, 3 * hd), cum, c_s, tile)
        q_cat, lat_pad, lat_new = _mla_prep(u_mla, u_small, cos_p, sin_p, lw['g_q'], lw['g_kv'], lw['w_nope'],
                                            lw['w_rope'], lw['w_rope_sw'], lw['w_uk_bd'], tm, rope)
        br_c = _mla_attn(q_cat.reshape(b, t, -1), lat_pad.reshape(b, t, -1), lw['w_uv_bd'], tile, kv_lora, mla_scale)
        br_d = _sb_attn(u_sb.reshape(b, t, 3 * hd), sb_tile)
        branches = [br_a.reshape(np_rows, dp), br_b.reshape(np_rows, hd), br_c.reshape(np_rows, hd),
                    br_d.reshape(np_rows, hd)]
        x1_p, h2, ti_p, tp_p, rk_p, counts = _merge(
            xp, mod_p, branches, lw['w_gate'], lw['w_branch'], lw['w_out'], lw['ln1_g'], lw['ln1_b'],
            lw['w_router'], lw['b_router'], zero_counts, h2_buf, 0, tm, t, alpha)
        outs['fox_p'].append(u_fox[:, hd:].reshape(b, t, 2, N_HEADS, dh))
        outs['lf_p'].append(lf)
        outs['mla_p'].append(lat_new.reshape(b, t, kv_lora + rope))
        outs['sb_p'].append(u_sb[:, hd:].reshape(b, t, 2, N_HEADS, dh))
        outs['pool_p'].append(pool_new)

        u_pool, u_fox, u_mla, u_sb, u_small = _inproj(xs, mod_s, lw['w_a'], lw['widths'], ns_rows, dt)
        br_a, pool_new = _pool(u_pool.reshape(db, dt, dp), state_pool[l], lw['w_pool_bd'], lw['pool_scale'], past)
        lf, cum = _fox_prep(u_small.reshape(db, dt, LANES), lw['fox_bias'], 2 * rope)
        u_fox3 = u_fox.reshape(db, dt, 3 * hd)
        g_new = jnp.broadcast_to(-jnp.transpose(cum, (0, 2, 1))[:, :, None, :], (db, N_HEADS, DEC_ROWS, dt))
        g_new = jnp.pad(g_new.reshape(db, N_HEADS * DEC_ROWS, dt), ((0, 0), (0, 0), (0, LANES - dt)))
        o_fox = _fox_dec(l, _head_rows(u_fox3[:, :, :hd], dh ** -0.5), u_fox3[:, :, hd:], g_new,
                         fox_kv_t, sfx, page_table, dt)
        q_cat, lat_pad, lat_new = _mla_prep(u_mla, u_small, cos_s, sin_s, lw['g_q'], lw['g_kv'], lw['w_nope'],
                                            lw['w_rope'], lw['w_rope_sw'], lw['w_uk_bd'], ns_rows, rope)
        wq = kv_lora + LANES
        q_rows = q_cat.reshape(db, dt, N_HEADS, wq)[..., :kv_lora + rope]
        q_rows = jnp.transpose(q_rows, (0, 2, 1, 3)).reshape(db, N_HEADS * dt, kv_lora + rope)
        o_mla = _mla_dec(l, q_rows, lat_new.reshape(db, dt, -1), lw['w_uv_all'], mla_t, page_table, dt, kv_lora,
                         mla_scale)
        u_sb3 = u_sb.reshape(db, dt, 3 * hd)
        o_sb = _sb_dec(l, _head_rows(u_sb3[:, :, :hd], dh ** -0.5), u_sb3[:, :, hd:],
                       sb_kv_t, page_table, dt)
        branches = [br_a.reshape(ns_rows, dp), _head_out(o_fox, dt), _diag_heads(o_mla, dt), _head_out(o_sb, dt)]
        x1_s, h2, ti_s, tp_s, rk_s, counts = _merge(
            xs, mod_s, branches, lw['w_gate'], lw['w_branch'], lw['w_out'], lw['ln1_g'], lw['ln1_b'],
            lw['w_router'], lw['b_router'], counts, h2, np_rows, ns_rows, dt, alpha)
        outs['fox_s'].append(u_fox3[:, :, hd:].reshape(db, dt, 2, N_HEADS, dh))
        outs['lf_s'].append(lf)
        outs['mla_s'].append(lat_new.reshape(db, dt, kv_lora + rope))
        outs['sb_s'].append(u_sb3[:, :, hd:].reshape(db, dt, 2, N_HEADS, dh))
        outs['pool_s'].append(pool_new)

        slot_tok, dest, block_exp, n_used = _routing(jnp.concatenate([ti_p, ti_s], axis=0),
                                                     jnp.concatenate([rk_p, rk_s], axis=0), counts)
        y_slots = _experts(l, block_exp, n_used, h2[slot_tok], w_exp1, lw['b1g'], lw['b1l'], w_exp2, lw['b2'])
        h2_buf = h2
        y_p = [y_slots[dest[:np_rows, k]] for k in range(TOP_K)]
        y_s = [y_slots[dest[np_rows:, k]] for k in range(TOP_K)]
        xp = _final(x1_p, mod_p, y_p, tp_p, lw['ln2_g'], lw['ln2_b'], tm, t, alpha)
        xs = _final(x1_s, mod_s, y_s, tp_s, lw['ln2_g'], lw['ln2_b'], ns_rows, dt, alpha)

    st = lambda k: jnp.stack(outs[k])
    return (xp.reshape(b, t, d), xs.reshape(db, dt, d), st('fox_p'), st('fox_s'), st('lf_p'), st('lf_s'),
            st('mla_p'), st('mla_s'), st('sb_p'), st('sb_s'), st('pool_p'), st('pool_s'))
```
